```python
import jax, jax.numpy as jnp
from jax import lax
import numpy as np

D_MODEL = 1024
BATCH = 32
SEQ = 2048
DEPTH = 4
DEC_BATCH = 8
DEC_SEQ = 64
PAST_LEN = 1024

CHUNK = 64
N_MIXERS = 3
N_ATTN = (DEPTH + N_MIXERS - 1) // N_MIXERS
N_HEADS = 8
HEAD_DIM = D_MODEL // N_HEADS
N_KV_HEADS = 2
GROUP = N_HEADS // N_KV_HEADS
IDX_HEADS = 8
IDX_DIM = 64
TOPK_MAX = 256
Q_BLOCK = 128
ROPE_THETA = 10000.0
ATTN_PROJ = (N_HEADS + 2 * N_KV_HEADS) * HEAD_DIM + IDX_HEADS * IDX_DIM + IDX_DIM + IDX_HEADS
CONV_WIDTH = 3
RW_HEAD = 64
RW_HEADS = D_MODEL // RW_HEAD
DECAY_LORA = 64
AAA_LORA = 64
GATE_LORA = 128
RW_GN_EPS = 64e-5
D_FF = 2 * D_MODEL
NORM_EPS = 1e-6

kernel_name = 'chunk_causal_dsa_conv_rwkv7_hybrid_step'


def rmsnorm(x, g):
    xf = x.astype(jnp.float32)
    xf = xf * lax.rsqrt(jnp.mean(xf * xf, axis=-1, keepdims=True) + NORM_EPS)
    return (xf * g.astype(jnp.float32)).astype(x.dtype)


def rope(x, pos):
    d = x.shape[-1]
    half = d // 2
    inv = ROPE_THETA ** (-2.0 * jnp.arange(half, dtype=jnp.float32) / d)
    ang = pos.astype(jnp.float32)[:, None] * inv[None, :]
    cos = jnp.cos(ang)[:, None, :]
    sin = jnp.sin(ang)[:, None, :]
    xf = x.astype(jnp.float32)
    x1, x2 = xf[..., :half], xf[..., half:]
    return jnp.concatenate([x1 * cos - x2 * sin, x2 * cos + x1 * sin], axis=-1).astype(x.dtype)


def causal_conv(xp, w):
    T = xp.shape[1] - (CONV_WIDTH - 1)
    return sum(w[j] * xp[:, j:j + T] for j in range(CONV_WIDTH))


def dsa_core(q, qi, wi, qpos, k, v, ki, kpos, n_sel):
    tq = q.shape[0]
    admissible = (kpos[None, :] // CHUNK) <= (qpos[:, None] // CHUNK)
    logits = jnp.einsum('thd,sd->ths', qi, ki).astype(jnp.float32) * IDX_DIM ** -0.5
    score = jnp.einsum('th,ths->ts', wi.astype(jnp.float32), jax.nn.relu(logits))
    score = jnp.where(admissible, score, -jnp.inf)
    _, idx = lax.top_k(score, n_sel)
    valid = jnp.take_along_axis(admissible, idx, axis=1)
    ks = k[idx]
    vs = v[idx]
    qg = q.reshape(tq, N_KV_HEADS, GROUP, HEAD_DIM)
    s = jnp.einsum('tkgd,tnkd->tkgn', qg, ks).astype(jnp.float32) * HEAD_DIM ** -0.5
    s = jnp.where(valid[:, None, None, :], s, -jnp.inf)
    p = jax.nn.softmax(s, axis=-1).astype(vs.dtype)
    o = jnp.einsum('tkgn,tnkd->tkgd', p, vs)
    return o.reshape(tq, N_HEADS * HEAD_DIM)


def dsa_mixer(h, w_in, w_out, past_k, past_v, past_ki):
    B, T, _ = h.shape
    P = past_k.shape[1]
    L = P + T
    proj = h @ w_in
    cuts = [N_HEADS * HEAD_DIM, (N_HEADS + N_KV_HEADS) * HEAD_DIM, (N_HEADS + 2 * N_KV_HEADS) * HEAD_DIM]
    cuts = cuts + [cuts[-1] + IDX_HEADS * IDX_DIM, cuts[-1] + IDX_HEADS * IDX_DIM + IDX_DIM]
    q, k, v, qi, ki, wi = jnp.split(proj, cuts, axis=-1)
    pos = jnp.arange(P, L, dtype=jnp.int32)
    q = rope(q.reshape(B, T, N_HEADS, HEAD_DIM), pos)
    k = rope(k.reshape(B, T, N_KV_HEADS, HEAD_DIM), pos)
    v = v.reshape(B, T, N_KV_HEADS, HEAD_DIM)
    qi = rope(qi.reshape(B, T, IDX_HEADS, IDX_DIM), pos)
    ki = rope(ki.reshape(B, T, 1, IDX_DIM), pos)[:, :, 0]
    wi = wi * IDX_HEADS ** -0.5
    k_all = jnp.concatenate([past_k, k], axis=1)
    v_all = jnp.concatenate([past_v, v], axis=1)
    ki_all = jnp.concatenate([past_ki, ki], axis=1)
    kpos = jnp.arange(L, dtype=jnp.int32)
    n_sel = min(TOPK_MAX, L // 4)
    q_blk = Q_BLOCK if T % Q_BLOCK == 0 else T
    nb = T // q_blk

    def block(args):
        qb, qib, wib, b, start = args
        qpos = P + start + jnp.arange(q_blk, dtype=jnp.int32)
        return dsa_core(qb, qib, wib, qpos, k_all[b], v_all[b], ki_all[b], kpos, n_sel)

    o = lax.map(block, (q.reshape(B * nb, q_blk, N_HEADS, HEAD_DIM),
                        qi.reshape(B * nb, q_blk, IDX_HEADS, IDX_DIM),
                        wi.reshape(B * nb, q_blk, IDX_HEADS),
                        jnp.repeat(jnp.arange(B, dtype=jnp.int32), nb),
                        jnp.tile(jnp.arange(nb, dtype=jnp.int32) * q_blk, B)))
    out = o.reshape(B, T, N_HEADS * HEAD_DIM) @ w_out
    return out, k, v, ki


def short_conv_mixer(h, left, w_in, conv_w, w_out):
    gb, gc, u = jnp.split(h @ w_in, 3, axis=-1)
    u = gc * u
    up = jnp.concatenate([left, u], axis=1)
    y = causal_conv(up, conv_w)
    return (gb * y) @ w_out, up[:, -(CONV_WIDTH - 1):]


def rwkv7_mixer(h, shift_prev, wkv0, mix, w_rkv, w_o, w0, w1, w2, a0, a1, a2, g1, g2,
                k_k, k_a, r_k, ln_g, ln_b):
    B, T, D = h.shape
    f32 = jnp.float32
    xx = jnp.concatenate([shift_prev, h[:, :-1]], axis=1) - h
    xr, xw, xk, xv, xa, xg = (h + xx * mix[j] for j in range(6))
    r = xr @ w_rkv[0]
    k = xk @ w_rkv[1]
    v = xv @ w_rkv[2]
    w_log = -jax.nn.softplus(-(w0 + jnp.tanh(xw @ w1) @ w2).astype(f32)) - 0.5
    decay = jnp.exp(-jnp.exp(w_log))
    a = jax.nn.sigmoid((a0 + (xa @ a1) @ a2).astype(f32))
    g = jax.nn.sigmoid(xg @ g1) @ g2

    def heads(t):
        return t.astype(f32).reshape(B, T, RW_HEADS, RW_HEAD)

    r, k, v, decay, a = heads(r), heads(k), heads(v), heads(decay), heads(a)
    kk = k * k_k.astype(f32).reshape(RW_HEADS, RW_HEAD)
    kk = kk / jnp.maximum(jnp.sqrt(jnp.sum(kk * kk, axis=-1, keepdims=True)), 1e-12)
    k = k * (1.0 + (a - 1.0) * k_a.astype(f32).reshape(RW_HEADS, RW_HEAD))

    def step(S, inp):
        r_t, w_t, k_t, v_t, kk_t, a_t = inp
        sa = jnp.einsum('bhvk,bhk->bhv', S, -kk_t)
        S = S * w_t[:, :, None, :] + sa[..., None] * (kk_t * a_t)[:, :, None, :] + v_t[..., None] * k_t[:, :, None, :]
        return S, jnp.einsum('bhvk,bhk->bhv', S, r_t)

    def tm(t):
        return jnp.moveaxis(t, 1, 0)

    s_fin, y = lax.scan(step, wkv0.astype(f32), (tm(r), tm(decay), tm(k), tm(v), tm(kk), tm(a)))
    y = jnp.moveaxis(y, 0, 1)
    mu = jnp.mean(y, axis=-1, keepdims=True)
    var = jnp.mean(jnp.square(y - mu), axis=-1, keepdims=True)
    y = ((y - mu) * lax.rsqrt(var + RW_GN_EPS)).reshape(B, T, D) * ln_g.astype(f32) + ln_b.astype(f32)
    bonus = jnp.sum(r * k * r_k.astype(f32), axis=-1, keepdims=True) * v
    y = y + bonus.reshape(B, T, D)
    out = (y.astype(h.dtype) * g) @ w_o
    return out, h[:, -1:], s_fin.astype(wkv0.dtype)


def conv_ffn(h, left, w_up, conv_w, conv_b, w_down):
    u = h @ w_up
    up = jnp.concatenate([left, u], axis=1)
    z = causal_conv(up, conv_w) + conv_b
    gate, val = jnp.split(z, 2, axis=-1)
    return (jax.nn.silu(gate) * val) @ w_down, up[:, -(CONV_WIDTH - 1):]


def trunk(x, c, layer_states, ffn_left, prm):
    new_states = []
    new_ffn = []
    for i in range(DEPTH):
        mod = c @ prm['w_mod'][i] + prm['b_mod'][i]
        sh_a, sc_a, gt_a, sh_f, sc_f, gt_f = jnp.split(mod[:, None, :], 6, axis=-1)
        h = rmsnorm(x, prm['g_norm'][i, 0]) * (1.0 + sc_a) + sh_a
        kind = i % N_MIXERS
        if kind == 0:
            j = i // N_MIXERS
            out, *st = dsa_mixer(h, prm['attn_w_in'][j], prm['attn_w_out'][j], *layer_states[i])
        elif kind == 1:
            out, *st = short_conv_mixer(h, layer_states[i][0], prm['sc_w_in'], prm['sc_conv_w'], prm['sc_w_out'])
        else:
            out, *st = rwkv7_mixer(h, layer_states[i][0], layer_states[i][1], prm['rw_mix'], prm['rw_w_rkv'],
                                   prm['rw_w_o'], prm['rw_w0'], prm['rw_w1'], prm['rw_w2'], prm['rw_a0'],
                                   prm['rw_a1'], prm['rw_a2'], prm['rw_g1'], prm['rw_g2'], prm['rw_k_k'],
                                   prm['rw_k_a'], prm['rw_r_k'], prm['rw_ln_g'], prm['rw_ln_b'])
        new_states.append(tuple(st))
        x = x + gt_a * out
        h = rmsnorm(x, prm['g_norm'][i, 1]) * (1.0 + sc_f) + sh_f
        out, f_st = conv_ffn(h, ffn_left[i], prm['ffn_w_up'][i], prm['ffn_conv_w'][i],
                             prm['ffn_conv_b'][i], prm['ffn_w_down'][i])
        new_ffn.append(f_st)
        x = x + gt_f * out
    return rmsnorm(x, prm['g_final']), new_states, jnp.stack(new_ffn)


def empty_layer_states(b, dt):
    out = []
    for i in range(DEPTH):
        kind = i % N_MIXERS
        if kind == 0:
            out.append((jnp.zeros((b, 0, N_KV_HEADS, HEAD_DIM), dt), jnp.zeros((b, 0, N_KV_HEADS, HEAD_DIM), dt),
                        jnp.zeros((b, 0, IDX_DIM), dt)))
        elif kind == 1:
            out.append((jnp.zeros((b, CONV_WIDTH - 1, D_MODEL), dt),))
        else:
            out.append((jnp.zeros((b, 1, D_MODEL), dt), jnp.zeros((b, RW_HEADS, RW_HEAD, RW_HEAD), dt)))
    return out


def setup_inputs(seed: int = 0) -> dict:
    key = jax.random.key(seed)
    ks = iter(jax.random.split(key, 48))
    D = D_MODEL

    def nrm(shape, scale=1.0):
        return scale * jax.random.normal(next(ks), shape, jnp.float32)

    inp = {}
    inp['x_prompt'] = nrm((BATCH, SEQ, D))
    inp['x_sample'] = nrm((DEC_BATCH, DEC_SEQ, D))
    inp['c_prompt'] = nrm((BATCH, D))
    inp['c_sample'] = nrm((DEC_BATCH, D))
    inp['cache_k_0'] = nrm((DEC_BATCH, PAST_LEN, N_KV_HEADS, HEAD_DIM))
    inp['cache_v_0'] = nrm((DEC_BATCH, PAST_LEN, N_KV_HEADS, HEAD_DIM))
    inp['cache_kidx_0'] = nrm((DEC_BATCH, PAST_LEN, IDX_DIM))
    inp['state_conv_1'] = nrm((DEC_BATCH, CONV_WIDTH - 1, D), 0.5)
    inp['state_shift_2'] = nrm((DEC_BATCH, 1, D))
    inp['state_wkv_2'] = nrm((DEC_BATCH, RW_HEADS, RW_HEAD, RW_HEAD), 0.3)
    inp['cache_k_3'] = nrm((DEC_BATCH, PAST_LEN, N_KV_HEADS, HEAD_DIM))
    inp['cache_v_3'] = nrm((DEC_BATCH, PAST_LEN, N_KV_HEADS, HEAD_DIM))
    inp['cache_kidx_3'] = nrm((DEC_BATCH, PAST_LEN, IDX_DIM))
    inp['state_ffn_conv'] = nrm((DEPTH, DEC_BATCH, CONV_WIDTH - 1, 2 * D_FF))
    inp['w_mod'] = nrm((DEPTH, D, 6 * D), 0.3 * D ** -0.5)
    inp['b_mod'] = nrm((DEPTH, 6 * D), 0.1)
    inp['g_norm'] = 1.0 + nrm((DEPTH, 2, D), 0.05)
    inp['g_final'] = 1.0 + nrm((D,), 0.05)
    inp['attn_w_in'] = nrm((N_ATTN, D, ATTN_PROJ), D ** -0.5)
    inp['attn_w_out'] = nrm((N_ATTN, N_HEADS * HEAD_DIM, D), (N_HEADS * HEAD_DIM) ** -0.5)
    inp['sc_w_in'] = nrm((D, 3 * D), D ** -0.5)
    inp['sc_conv_w'] = nrm((CONV_WIDTH, D), CONV_WIDTH ** -0.5)
    inp['sc_w_out'] = nrm((D, D), D ** -0.5)
    inp['rw_mix'] = jax.random.uniform(next(ks), (6, D), jnp.float32)
    inp['rw_w_rkv'] = nrm((3, D, D), D ** -0.5)
    inp['rw_w_o'] = nrm((D, D), D ** -0.5)
    inp['rw_w0'] = nrm((D,), 0.5)
    inp['rw_w1'] = nrm((D, DECAY_LORA), D ** -0.5)
    inp['rw_w2'] = nrm((DECAY_LORA, D), 0.1 * DECAY_LORA ** -0.5)
    inp['rw_a0'] = nrm((D,), 0.1)
    inp['rw_a1'] = nrm((D, AAA_LORA), D ** -0.5)
    inp['rw_a2'] = nrm((AAA_LORA, D), 0.1 * AAA_LORA ** -0.5)
    inp['rw_g1'] = nrm((D, GATE_LORA), D ** -0.5)
    inp['rw_g2'] = nrm((GATE_LORA, D), GATE_LORA ** -0.5)
    inp['rw_k_k'] = 0.85 + nrm((D,), 0.05)
    inp['rw_k_a'] = 1.0 + nrm((D,), 0.05)
    inp['rw_r_k'] = nrm((RW_HEADS, RW_HEAD), 0.1)
    inp['rw_ln_g'] = 1.0 + nrm((D,), 0.05)
    inp['rw_ln_b'] = nrm((D,), 0.02)
    inp['ffn_w_up'] = nrm((DEPTH, D, 2 * D_FF), D ** -0.5)
    inp['ffn_conv_w'] = nrm((DEPTH, CONV_WIDTH, 2 * D_FF), CONV_WIDTH ** -0.5)
    inp['ffn_conv_b'] = nrm((DEPTH, 2 * D_FF), 0.02)
    inp['ffn_w_down'] = nrm((DEPTH, D_FF, D), D_FF ** -0.5)
    return inp


def reference(x_prompt, x_sample, c_prompt, c_sample, cache_k_0, cache_v_0, cache_kidx_0, state_conv_1,
              state_shift_2, state_wkv_2, cache_k_3, cache_v_3, cache_kidx_3, state_ffn_conv,
              w_mod, b_mod, g_norm, g_final, attn_w_in, attn_w_out, sc_w_in, sc_conv_w, sc_w_out,
              rw_mix, rw_w_rkv, rw_w_o, rw_w0, rw_w1, rw_w2, rw_a0, rw_a1, rw_a2, rw_g1, rw_g2,
              rw_k_k, rw_k_a, rw_r_k, rw_ln_g, rw_ln_b, ffn_w_up, ffn_conv_w, ffn_conv_b, ffn_w_down):
    prm = {'w_mod': w_mod, 'b_mod': b_mod, 'g_norm': g_norm, 'g_final': g_final,
           'attn_w_in': attn_w_in, 'attn_w_out': attn_w_out,
           'sc_w_in': sc_w_in, 'sc_conv_w': sc_conv_w, 'sc_w_out': sc_w_out,
           'rw_mix': rw_mix, 'rw_w_rkv': rw_w_rkv, 'rw_w_o': rw_w_o, 'rw_w0': rw_w0, 'rw_w1': rw_w1,
           'rw_w2': rw_w2, 'rw_a0': rw_a0, 'rw_a1': rw_a1, 'rw_a2': rw_a2, 'rw_g1': rw_g1, 'rw_g2': rw_g2,
           'rw_k_k': rw_k_k, 'rw_k_a': rw_k_a, 'rw_r_k': rw_r_k, 'rw_ln_g': rw_ln_g, 'rw_ln_b': rw_ln_b,
           'ffn_w_up': ffn_w_up, 'ffn_conv_w': ffn_conv_w, 'ffn_conv_b': ffn_conv_b, 'ffn_w_down': ffn_w_down}
    bp = x_prompt.shape[0]
    dt = x_prompt.dtype
    ffn0 = jnp.zeros((DEPTH, bp, CONV_WIDTH - 1, 2 * D_FF), dt)
    y_prompt, st_p, ffn_p = trunk(x_prompt, c_prompt, empty_layer_states(bp, dt), ffn0, prm)
    sample_states = [(cache_k_0, cache_v_0, cache_kidx_0), (state_conv_1,), (state_shift_2, state_wkv_2),
                     (cache_k_3, cache_v_3, cache_kidx_3)]
    y_sample, st_s, ffn_s = trunk(x_sample, c_sample, sample_states, state_ffn_conv, prm)
    (k0_p, v0_p, ki0_p), (conv1_p,), (shift2_p, wkv2_p), (k3_p, v3_p, ki3_p) = st_p
    (k0_s, v0_s, ki0_s), (conv1_s,), (shift2_s, wkv2_s), (k3_s, v3_s, ki3_s) = st_s
    return (y_prompt, y_sample,
            k0_p, v0_p, ki0_p, conv1_p, shift2_p, wkv2_p, k3_p, v3_p, ki3_p, ffn_p,
            k0_s, v0_s, ki0_s, conv1_s, shift2_s, wkv2_s, k3_s, v3_s, ki3_s, ffn_s)
```

```python
import functools

import jax
import jax.numpy as jnp
from jax import lax
from jax.experimental import pallas as pl
from jax.experimental.pallas import tpu as pltpu

D_MODEL = 1024
DEPTH = 4
CHUNK = 64
N_HEADS = 8
HEAD_DIM = 128
N_KV_HEADS = 2
GROUP = N_HEADS // N_KV_HEADS
IDX_HEADS = 8
IDX_DIM = 64
TOPK_MAX = 256
ROPE_THETA = 10000.0
RW_HEAD = 64
RW_HEADS = D_MODEL // RW_HEAD
RW_GN_EPS = 64e-5
D_FF = 2 * D_MODEL
NORM_EPS = 1e-6

LANES = 128
SUBLANES = 8
VMEM_LIMIT = 52 * 1024 * 1024
INT_MIN = -2 ** 31

F32 = jnp.float32
BF16 = jnp.bfloat16

Q_END = N_HEADS * HEAD_DIM
K_END = Q_END + N_KV_HEADS * HEAD_DIM
V_END = K_END + N_KV_HEADS * HEAD_DIM
QI_END = V_END + IDX_HEADS * IDX_DIM
KI_END = QI_END + IDX_DIM
WI_END = KI_END + IDX_HEADS


def _cparams(sem):
    return pltpu.CompilerParams(dimension_semantics=sem, vmem_limit_bytes=VMEM_LIMIT)


def _dot(a, b):
    return jnp.dot(a, b, preferred_element_type=F32)


def _dot_nt(a, b):
    return lax.dot_general(a, b, (((1,), (1,)), ((), ())), preferred_element_type=F32)


def _norm_mod(x, g, scale, shift):
    ms = jnp.mean(x * x, axis=-1, keepdims=True)
    return (x * lax.rsqrt(ms + NORM_EPS) * g) * (1.0 + scale) + shift


def _shifted(u, prevs):
    n = len(prevs)
    row = lax.broadcasted_iota(jnp.int32, (SUBLANES, 1), 0)
    outs = []
    for s in range(n, 0, -1):
        r = pltpu.roll(u, s, 0)
        r8 = r[0:SUBLANES]
        for i in range(s):
            r8 = jnp.where(row == i, prevs[n - s + i], r8)
        outs.append(jnp.concatenate([r8, r[SUBLANES:]], axis=0))
    return outs


def _causal_conv3(u, c0, c1, w0, w1, w2):
    s2, s1 = _shifted(u, [c0, c1])
    return w0 * s2 + w1 * s1 + w2 * u


def _mod_kernel(c_ref, w_ref, b_ref, o_ref):
    o_ref[0] = _dot(c_ref[...].astype(BF16), w_ref[0].astype(BF16)) + b_ref[0]


def _mod_all(c_all, w_mod, b_mod):
    nb = c_all.shape[0]
    d = D_MODEL
    return pl.pallas_call(
        _mod_kernel,
        out_shape=jax.ShapeDtypeStruct((DEPTH, nb, 6 * d), F32),
        grid=(DEPTH, 6),
        in_specs=[pl.BlockSpec((nb, d), lambda l, n: (0, 0)),
                  pl.BlockSpec((1, d, d), lambda l, n: (l, 0, n)),
                  pl.BlockSpec((1, 1, d), lambda l, n: (l, 0, n))],
        out_specs=pl.BlockSpec((1, nb, d), lambda l, n: (l, 0, n)),
        compiler_params=_cparams(("parallel", "parallel")),
    )(c_all, w_mod, b_mod.reshape(DEPTH, 1, 6 * d))


def _rope128(seg, cos, sin_signed):
    return seg * cos + pltpu.roll(seg, HEAD_DIM // 2, 1) * sin_signed


def _rope64(seg, cos, sin_lo, sin_hi):
    return seg * cos + pltpu.roll(seg, LANES - IDX_DIM // 2, 1) * sin_lo + pltpu.roll(seg, IDX_DIM // 2, 1) * sin_hi


def _attn_proj_kernel(x_ref, mod_ref, g_ref, wm_ref, wt_ref, c128_ref, s128_ref, c64_ref, s64lo_ref,
                      s64hi_ref, ct_ref, stlo_ref, sthi_ref, q_ref, k_ref, v_ref, qi_ref, kw_ref):
    h = _norm_mod(x_ref[0], g_ref[...], mod_ref[0, 1:2, :], mod_ref[0, 0:1, :]).astype(BF16)
    pm = _dot(h, wm_ref[...])
    c128 = c128_ref[...]
    s128 = s128_ref[...]
    for hh in range(N_HEADS):
        lo = hh * HEAD_DIM
        q_ref[0, :, lo:lo + HEAD_DIM] = _rope128(pm[:, lo:lo + HEAD_DIM], c128, s128)
    for hh in range(N_KV_HEADS):
        lo = hh * HEAD_DIM
        k_ref[0, :, lo:lo + HEAD_DIM] = _rope128(pm[:, Q_END + lo:Q_END + lo + HEAD_DIM], c128, s128)
    v_ref[0] = pm[:, K_END:V_END]
    c64 = c64_ref[...]
    s64lo = s64lo_ref[...]
    s64hi = s64hi_ref[...]
    for cc in range(IDX_HEADS * IDX_DIM // LANES):
        lo = cc * LANES
        qi_ref[0, :, lo:lo + LANES] = _rope64(pm[:, V_END + lo:V_END + lo + LANES], c64, s64lo, s64hi)
    pt = _dot(h, wt_ref[...])
    kw_ref[0] = _rope64(pt, ct_ref[...], stlo_ref[...], sthi_ref[...])


def _rope_tables(t_len, past):
    pos = jnp.arange(past, past + t_len, dtype=jnp.int32).astype(F32)[:, None]

    def cs(d):
        half = d // 2
        inv = ROPE_THETA ** (-2.0 * jnp.arange(half, dtype=F32) / d)
        ang = pos * inv[None, :]
        return jnp.cos(ang), jnp.sin(ang)

    c, s = cs(HEAD_DIM)
    c128 = jnp.concatenate([c, c], axis=1)
    s128 = jnp.concatenate([-s, s], axis=1)
    c, s = cs(IDX_DIM)
    z = jnp.zeros_like(s)
    c64 = jnp.concatenate([c, c, c, c], axis=1)
    s64lo = jnp.concatenate([-s, z, -s, z], axis=1)
    s64hi = jnp.concatenate([z, s, z, s], axis=1)
    wscale = jnp.full((t_len, LANES - IDX_DIM), IDX_HEADS ** -0.5, F32)
    ct = jnp.concatenate([c, c, wscale], axis=1)
    stlo = jnp.concatenate([-s, z, z, z], axis=1)
    sthi = jnp.concatenate([z, s, z, z], axis=1)
    return c128, s128, c64, s64lo, s64hi, ct, stlo, sthi


def _attn_proj(x, mod, g, w_main, w_tail, tables, tm):
    b, t, d = x.shape
    tab_spec = pl.BlockSpec((tm, LANES), lambda bi, i: (i, 0))
    const2 = lambda bi, i: (0, 0)
    row3 = lambda bi, i: (bi, i, 0)
    return pl.pallas_call(
        _attn_proj_kernel,
        out_shape=(jax.ShapeDtypeStruct((b, t, Q_END), F32),
                   jax.ShapeDtypeStruct((b, t, K_END - Q_END), F32),
                   jax.ShapeDtypeStruct((b, t, V_END - K_END), F32),
                   jax.ShapeDtypeStruct((b, t, QI_END - V_END), F32),
                   jax.ShapeDtypeStruct((b, t, LANES), F32)),
        grid=(b, t // tm),
        in_specs=[pl.BlockSpec((1, tm, d), row3),
                  pl.BlockSpec((1, 6, d), lambda bi, i: (bi, 0, 0)),
                  pl.BlockSpec((1, d), const2),
                  pl.BlockSpec((d, QI_END), const2),
                  pl.BlockSpec((d, LANES), const2)] + [tab_spec] * 8,
        out_specs=(pl.BlockSpec((1, tm, Q_END), row3),
                   pl.BlockSpec((1, tm, K_END - Q_END), row3),
                   pl.BlockSpec((1, tm, V_END - K_END), row3),
                   pl.BlockSpec((1, tm, QI_END - V_END), row3),
                   pl.BlockSpec((1, tm, LANES), row3)),
        compiler_params=_cparams(("parallel", "parallel")),
    )(x, mod, g, w_main, w_tail, *tables)


def _attn_core_kernel(q_ref, qi_ref, kw_ref, k_ref, v_ref, ki_ref, x_ref, mod_ref, wo_ref, o_ref,
                      *, past, l_true, n_sel):
    tq = q_ref.shape[1]
    lp = k_ref.shape[1]
    i = pl.program_id(1)
    qpos = past + i * tq + lax.broadcasted_iota(jnp.int32, (tq, 1), 0)
    kidx = lax.broadcasted_iota(jnp.int32, (tq, lp), 1)
    adm = (lax.shift_right_logical(kidx, 6) <= lax.shift_right_logical(qpos, 6)) & (kidx < l_true)

    kib = ki_ref[0].astype(BF16)
    score = jnp.zeros((tq, lp), F32)
    for hh in range(IDX_HEADS):
        qh = qi_ref[0, :, hh * IDX_DIM:(hh + 1) * IDX_DIM].astype(BF16)
        lg = _dot_nt(qh, kib)
        wi = kw_ref[0, :, IDX_DIM + hh:IDX_DIM + hh + 1]
        score = score + wi * jnp.maximum(lg, 0.0)

    bits = lax.bitcast_convert_type(score + 0.0, jnp.int32)
    key = bits ^ ((bits >> 31) & jnp.int32(0x7FFFFFFF))
    key = jnp.where(adm, key, jnp.int32(INT_MIN))
    kf = jnp.float32(n_sel)

    def thr_body(it, t):
        cand = t + lax.shift_left(jnp.int32(1), 31 - it)
        cnt = jnp.sum(jnp.where(key >= cand, 1.0, 0.0), axis=1, keepdims=True)
        return jnp.where(cnt >= kf, cand, t)

    thr = lax.fori_loop(0, 32, thr_body, jnp.full((tq, 1), INT_MIN, jnp.int32))
    gt = key > thr
    eq = (key == thr) & adm
    need = kf - jnp.sum(jnp.where(gt, 1.0, 0.0), axis=1, keepdims=True)
    eqf = jnp.where(eq, 1.0, 0.0)
    excess = jnp.sum(eqf, axis=1, keepdims=True) - need

    def tie_search():
        nbits = lp.bit_length()

        def body(it, m):
            cand = m + lax.shift_left(jnp.int32(1), nbits - 1 - it)
            f = jnp.sum(jnp.where(kidx < cand, eqf, 0.0), axis=1, keepdims=True)
            return jnp.where(f < need, cand, m)

        return lax.fori_loop(0, nbits, body, jnp.zeros((tq, 1), jnp.int32)) + 1

    jstar = lax.cond(jnp.max(excess) > 0.0, tie_search, lambda: jnp.full((tq, 1), lp, jnp.int32))
    sel = gt | (eq & (kidx < jstar))

    outs = []
    for kv in range(N_KV_HEADS):
        kk = k_ref[0, :, kv * HEAD_DIM:(kv + 1) * HEAD_DIM].astype(BF16)
        vv = v_ref[0, :, kv * HEAD_DIM:(kv + 1) * HEAD_DIM].astype(BF16)
        for gi in range(GROUP):
            hq = kv * GROUP + gi
            qh = q_ref[0, :, hq * HEAD_DIM:(hq + 1) * HEAD_DIM].astype(BF16)
            s = _dot_nt(qh, kk) * HEAD_DIM ** -0.5
            s = jnp.where(sel, s, -jnp.inf)
            m = jnp.max(s, axis=1, keepdims=True)
            p = jnp.exp(s - m)
            l = jnp.sum(p, axis=1, keepdims=True)
            outs.append(_dot(p.astype(BF16), vv) / l)
    o = jnp.concatenate(outs, axis=1).astype(BF16)
    o_ref[0] = x_ref[0] + mod_ref[0, 2:3, :] * _dot(o, wo_ref[...])


def _attn_core(q, qi, kw, k_all, v_all, ki_all, x, mod, w_out, *, past, l_true, tq):
    b, t, d = x.shape
    lp = k_all.shape[1]
    n_sel = min(TOPK_MAX, l_true // 4)
    row3 = lambda bi, i: (bi, i, 0)
    per_b = lambda bi, i: (bi, 0, 0)
    kern = functools.partial(_attn_core_kernel, past=past, l_true=l_true, n_sel=n_sel)
    return pl.pallas_call(
        kern,
        out_shape=jax.ShapeDtypeStruct((b, t, d), F32),
        grid=(b, t // tq),
        in_specs=[pl.BlockSpec((1, tq, Q_END), row3),
                  pl.BlockSpec((1, tq, QI_END - V_END), row3),
                  pl.BlockSpec((1, tq, LANES), row3),
                  pl.BlockSpec((1, lp, K_END - Q_END), per_b),
                  pl.BlockSpec((1, lp, V_END - K_END), per_b),
                  pl.BlockSpec((1, lp, IDX_DIM), per_b),
                  pl.BlockSpec((1, tq, d), row3),
                  pl.BlockSpec((1, 6, d), per_b),
                  pl.BlockSpec((N_HEADS * HEAD_DIM, d), lambda bi, i: (0, 0))],
        out_specs=pl.BlockSpec((1, tq, d), row3),
        compiler_params=_cparams(("parallel", "parallel")),
    )(q, qi, kw, k_all, v_all, ki_all, x, mod, w_out)


def _conv_mixer_kernel(x_ref, mod_ref, g_ref, win_ref, cw_ref, wout_ref, left_ref, o_ref, st_ref, carry_ref):
    d = D_MODEL
    tm = x_ref.shape[1]

    @pl.when(pl.program_id(1) == 0)
    def _():
        carry_ref[6:8, :] = left_ref[0]

    x = x_ref[0]
    h = _norm_mod(x, g_ref[...], mod_ref[0, 1:2, :], mod_ref[0, 0:1, :]).astype(BF16)
    gb = _dot(h, win_ref[:, 0:d])
    gc = _dot(h, win_ref[:, d:2 * d])
    u = gc * _dot(h, win_ref[:, 2 * d:3 * d])
    y = _causal_conv3(u, carry_ref[6:7, :], carry_ref[7:8, :], cw_ref[0:1, :], cw_ref[1:2, :], cw_ref[2:3, :])
    tail = u[tm - 2:tm, :]
    carry_ref[6:8, :] = tail
    st_ref[0] = tail
    o_ref[0] = x + mod_ref[0, 2:3, :] * _dot((gb * y).astype(BF16), wout_ref[...])


def _conv_mixer(x, mod, g, w_in, conv_w, w_out, left, tm):
    b, t, d = x.shape
    row3 = lambda bi, i: (bi, i, 0)
    per_b = lambda bi, i: (bi, 0, 0)
    const2 = lambda bi, i: (0, 0)
    return pl.pallas_call(
        _conv_mixer_kernel,
        out_shape=(jax.ShapeDtypeStruct((b, t, d), F32), jax.ShapeDtypeStruct((b, 2, d), F32)),
        grid=(b, t // tm),
        in_specs=[pl.BlockSpec((1, tm, d), row3),
                  pl.BlockSpec((1, 6, d), per_b),
                  pl.BlockSpec((1, d), const2),
                  pl.BlockSpec((d, 3 * d), const2),
                  pl.BlockSpec((3, d), const2),
                  pl.BlockSpec((d, d), const2),
                  pl.BlockSpec((1, 2, d), per_b)],
        out_specs=(pl.BlockSpec((1, tm, d), row3), pl.BlockSpec((1, 2, d), per_b)),
        scratch_shapes=[pltpu.VMEM((SUBLANES, d), F32)],
        compiler_params=_cparams(("parallel", "arbitrary")),
    )(x, mod, g, w_in, conv_w, w_out, left)


FFN_COLS = 512


def _ffn_kernel(x_ref, mod_ref, g_ref, wup_ref, cw_ref, cb_ref, wdn_ref, left_ref, gfin_ref, o_ref, st_ref,
                carry_ref, *, final_norm):
    tm = x_ref.shape[1]

    @pl.when(pl.program_id(1) == 0)
    def _():
        carry_ref[6:8, :] = left_ref[0]

    x = x_ref[0]
    h = _norm_mod(x, g_ref[...], mod_ref[0, 4:5, :], mod_ref[0, 3:4, :]).astype(BF16)

    def conv_cols(lo):
        u = _dot(h, wup_ref[:, lo:lo + FFN_COLS])
        z = _causal_conv3(u, carry_ref[6:7, lo:lo + FFN_COLS], carry_ref[7:8, lo:lo + FFN_COLS],
                          cw_ref[0:1, lo:lo + FFN_COLS], cw_ref[1:2, lo:lo + FFN_COLS],
                          cw_ref[2:3, lo:lo + FFN_COLS]) + cb_ref[:, lo:lo + FFN_COLS]
        tail = u[tm - 2:tm, :]
        carry_ref[6:8, lo:lo + FFN_COLS] = tail
        st_ref[0, :, lo:lo + FFN_COLS] = tail
        return z

    acc = jnp.zeros((tm, D_MODEL), F32)
    for c in range(0, D_FF, FFN_COLS):
        gate = conv_cols(c)
        val = conv_cols(D_FF + c)
        act = (gate * jax.nn.sigmoid(gate)) * val
        acc = acc + _dot(act.astype(BF16), wdn_ref[c:c + FFN_COLS, :])
    y = x + mod_ref[0, 5:6, :] * acc
    if final_norm:
        ms = jnp.mean(y * y, axis=-1, keepdims=True)
        y = y * lax.rsqrt(ms + NORM_EPS) * gfin_ref[...]
    o_ref[0] = y


def _ffn(x, mod, g, w_up, conv_w, conv_b, w_down, left, g_final, tm, final_norm):
    b, t, d = x.shape
    row3 = lambda bi, i: (bi, i, 0)
    per_b = lambda bi, i: (bi, 0, 0)
    const2 = lambda bi, i: (0, 0)
    kern = functools.partial(_ffn_kernel, final_norm=final_norm)
    return pl.pallas_call(
        kern,
        out_shape=(jax.ShapeDtypeStruct((b, t, d), F32), jax.ShapeDtypeStruct((b, 2, 2 * D_FF), F32)),
        grid=(b, t // tm),
        in_specs=[pl.BlockSpec((1, tm, d), row3),
                  pl.BlockSpec((1, 6, d), per_b),
                  pl.BlockSpec((1, d), const2),
                  pl.BlockSpec((d, 2 * D_FF), const2),
                  pl.BlockSpec((3, 2 * D_FF), const2),
                  pl.BlockSpec((1, 2 * D_FF), const2),
                  pl.BlockSpec((D_FF, d), const2),
                  pl.BlockSpec((1, 2, 2 * D_FF), per_b),
                  pl.BlockSpec((1, d), const2)],
        out_specs=(pl.BlockSpec((1, tm, d), row3), pl.BlockSpec((1, 2, 2 * D_FF), per_b)),
        scratch_shapes=[pltpu.VMEM((SUBLANES, 2 * D_FF), F32)],
        compiler_params=_cparams(("parallel", "arbitrary")),
    )(x, mod, g, w_up, conv_w, conv_b, w_down, left, g_final)


def _softplus(z):
    return jnp.maximum(z, 0.0) + jnp.log1p(jnp.exp(-jnp.abs(z)))


def _rwkv_proj_kernel(x_ref, mod_ref, g_ref, sp_ref, mix_ref, wrkv_ref, w1_ref, w2_ref, w0_ref, a1_ref, a2_ref,
                      a0_ref, g1_ref, g2_ref, r_ref, k_ref, v_ref, w_ref, a_ref, gg_ref, sh_ref, carry_ref):
    tm = x_ref.shape[1]

    @pl.when(pl.program_id(1) == 0)
    def _():
        carry_ref[7:8, :] = sp_ref[0]

    h = _norm_mod(x_ref[0], g_ref[...], mod_ref[0, 1:2, :], mod_ref[0, 0:1, :])
    (hs,) = _shifted(h, [carry_ref[7:8, :]])
    last = h[tm - 1:tm, :]
    carry_ref[7:8, :] = last
    sh_ref[0] = last
    xx = hs - h

    def mixed(j):
        return (h + xx * mix_ref[j:j + 1, :]).astype(BF16)

    r_ref[0] = _dot(mixed(0), wrkv_ref[0])
    k_ref[0] = _dot(mixed(2), wrkv_ref[1])
    v_ref[0] = _dot(mixed(3), wrkv_ref[2])
    lw = _dot(jnp.tanh(_dot(mixed(1), w1_ref[...])).astype(BF16), w2_ref[...])
    w_log = -_softplus(-(w0_ref[...] + lw)) - 0.5
    w_ref[0] = jnp.exp(-jnp.exp(w_log))
    a_ref[0] = jax.nn.sigmoid(a0_ref[...] + _dot(_dot(mixed(4), a1_ref[...]).astype(BF16), a2_ref[...]))
    gg_ref[0] = _dot(jax.nn.sigmoid(_dot(mixed(5), g1_ref[...])).astype(BF16), g2_ref[...])


def _rwkv_proj(x, mod, g, shift_prev, mix, w_rkv, w1, w2, w0, a1, a2, a0, g1, g2, tm):
    b, t, d = x.shape
    row3 = lambda bi, i: (bi, i, 0)
    per_b = lambda bi, i: (bi, 0, 0)
    const2 = lambda bi, i: (0, 0)
    big = jax.ShapeDtypeStruct((b, t, d), F32)
    lora = w1.shape[1]
    gl = g1.shape[1]
    return pl.pallas_call(
        _rwkv_proj_kernel,
        out_shape=(big,) * 6 + (jax.ShapeDtypeStruct((b, 1, d), F32),),
        grid=(b, t // tm),
        in_specs=[pl.BlockSpec((1, tm, d), row3),
                  pl.BlockSpec((1, 6, d), per_b),
                  pl.BlockSpec((1, d), const2),
                  pl.BlockSpec((1, 1, d), per_b),
                  pl.BlockSpec((6, d), const2),
                  pl.BlockSpec((3, d, d), lambda bi, i: (0, 0, 0)),
                  pl.BlockSpec((d, lora), const2),
                  pl.BlockSpec((lora, d), const2),
                  pl.BlockSpec((1, d), const2),
                  pl.BlockSpec((d, lora), const2),
                  pl.BlockSpec((lora, d), const2),
                  pl.BlockSpec((1, d), const2),
                  pl.BlockSpec((d, gl), const2),
                  pl.BlockSpec((gl, d), const2)],
        out_specs=(pl.BlockSpec((1, tm, d), row3),) * 6 + (pl.BlockSpec((1, 1, d), per_b),),
        scratch_shapes=[pltpu.VMEM((SUBLANES, d), F32)],
        compiler_params=_cparams(("parallel", "arbitrary")),
    )(x, mod, g, shift_prev, mix, w_rkv, w1, w2, w0, a1, a2, a0, g1, g2)


def _rwkv_scan_kernel(r_ref, k_ref, v_ref, w_ref, a_ref, kk_ref, ka_ref, rk_ref, lg_ref, lb_ref, s0_ref,
                      y_ref, sfin_ref, s_ref, bc_ref):
    tc = r_ref.shape[0]
    n = RW_HEAD

    @pl.when(pl.program_id(1) == 0)
    def _():
        s_ref[...] = s0_ref[...]

    def step(t, carry):
        kt = k_ref[t]
        at = a_ref[t]
        rt = r_ref[t]
        vt = v_ref[t]
        kkr = kt * kk_ref[...]
        nrm = jnp.sqrt(jnp.sum(kkr * kkr, axis=0, keepdims=True))
        kk = kkr / jnp.maximum(nrm, 1e-12)
        kmod = kt * (1.0 + (at - 1.0) * ka_ref[...])
        bc_ref[0] = -kk
        bc_ref[1] = w_ref[t]
        bc_ref[2] = kk * at
        bc_ref[3] = kmod
        bc_ref[4] = rt
        sa = jnp.zeros((n, LANES), F32)
        for j in range(n):
            sa = sa + s_ref[j] * bc_ref[0, j:j + 1, :]
        y = jnp.zeros((n, LANES), F32)
        for j in range(n):
            sj = s_ref[j] * bc_ref[1, j:j + 1, :] + sa * bc_ref[2, j:j + 1, :] + vt * bc_ref[3, j:j + 1, :]
            s_ref[j] = sj
            y = y + sj * bc_ref[4, j:j + 1, :]
        mu = jnp.mean(y, axis=0, keepdims=True)
        yc = y - mu
        var = jnp.mean(yc * yc, axis=0, keepdims=True)
        yn = (yc * lax.rsqrt(var + RW_GN_EPS)) * lg_ref[...] + lb_ref[...]
        bonus = jnp.sum(rt * kmod * rk_ref[...], axis=0, keepdims=True) * vt
        y_ref[t] = yn + bonus
        return carry

    lax.fori_loop(0, tc, step, 0)
    sfin_ref[...] = s_ref[...]


def _rwkv_scan(r, k, v, w, a, kk_t, ka_t, rk_t, lg_t, lb_t, s0, tc):
    t, n, bh = r.shape
    seq = pl.BlockSpec((tc, n, LANES), lambda lb, ti: (ti, 0, lb))
    par = pl.BlockSpec((n, LANES), lambda lb, ti: (0, lb))
    st = pl.BlockSpec((n, n, LANES), lambda lb, ti: (0, 0, lb))
    return pl.pallas_call(
        _rwkv_scan_kernel,
        out_shape=(jax.ShapeDtypeStruct((t, n, bh), F32), jax.ShapeDtypeStruct((n, n, bh), F32)),
        grid=(bh // LANES, t // tc),
        in_specs=[seq] * 5 + [par] * 5 + [st],
        out_specs=(seq, st),
        scratch_shapes=[pltpu.VMEM((n, n, LANES), F32), pltpu.VMEM((5, n, LANES), F32)],
        compiler_params=_cparams(("parallel", "arbitrary")),
    )(r, k, v, w, a, kk_t, ka_t, rk_t, lg_t, lb_t, s0)


def _rwkv_out_kernel(y_ref, gg_ref, x_ref, mod_ref, wo_ref, o_ref):
    o_ref[0] = x_ref[0] + mod_ref[0, 2:3, :] * _dot((y_ref[0] * gg_ref[0]).astype(BF16), wo_ref[...])


def _rwkv_out(y, gg, x, mod, w_o, tm):
    b, t, d = x.shape
    row3 = lambda bi, i: (bi, i, 0)
    return pl.pallas_call(
        _rwkv_out_kernel,
        out_shape=jax.ShapeDtypeStruct((b, t, d), F32),
        grid=(b, t // tm),
        in_specs=[pl.BlockSpec((1, tm, d), row3), pl.BlockSpec((1, tm, d), row3), pl.BlockSpec((1, tm, d), row3),
                  pl.BlockSpec((1, 6, d), lambda bi, i: (bi, 0, 0)),
                  pl.BlockSpec((d, d), lambda bi, i: (0, 0))],
        out_specs=pl.BlockSpec((1, tm, d), row3),
        compiler_params=_cparams(("parallel", "parallel")),
    )(y, gg, x, mod, w_o)


def _pad_rows(a, rows):
    return jnp.pad(a, ((0, 0), (0, rows - a.shape[1]), (0, 0)))


def _trunk(x, mods, states, ffn_left, wts, past):
    b, t, d = x.shape
    tm = min(t, 256)
    tq = min(t, 128)
    g_norm = wts['g_norm']
    new_states = []
    new_ffn = []
    tables = _rope_tables(t, past)
    for i in range(DEPTH):
        mod = mods[i]
        kind = i % 3
        g_a = g_norm[i, 0][None, :]
        if kind == 0:
            j = i // 3
            past_k, past_v, past_ki = states[i]
            q, k, v, qi, kw = _attn_proj(x, mod, g_a, wts['attn_w_main'][j], wts['attn_w_tail'][j], tables, tm)
            ki = kw[:, :, :IDX_DIM]
            l_true = past + t
            lp = -(-l_true // LANES) * LANES
            kvd = N_KV_HEADS * HEAD_DIM
            k_all = _pad_rows(jnp.concatenate([past_k.reshape(b, past, kvd), k], axis=1), lp)
            v_all = _pad_rows(jnp.concatenate([past_v.reshape(b, past, kvd), v], axis=1), lp)
            ki_all = _pad_rows(jnp.concatenate([past_ki, ki], axis=1), lp)
            x = _attn_core(q, qi, kw, k_all, v_all, ki_all, x, mod, wts['attn_w_out'][j],
                           past=past, l_true=l_true, tq=tq)
            new_states.append((k.reshape(b, t, N_KV_HEADS, HEAD_DIM), v.reshape(b, t, N_KV_HEADS, HEAD_DIM), ki))
        elif kind == 1:
            x, conv_st = _conv_mixer(x, mod, g_a, wts['sc_w_in'], wts['sc_conv_w'], wts['sc_w_out'],
                                     states[i][0], tm)
            new_states.append((conv_st,))
        else:
            shift_prev, wkv0 = states[i]
            r, k, v, w, a, gg, shift_new = _rwkv_proj(
                x, mod, g_a, shift_prev, wts['rw_mix'], wts['rw_w_rkv'], wts['rw_w1'], wts['rw_w2'], wts['rw_w0'],
                wts['rw_a1'], wts['rw_a2'], wts['rw_a0'], wts['rw_g1'], wts['rw_g2'], tm)
            bh = b * RW_HEADS

            def to_scan(z):
                return z.reshape(b, t, RW_HEADS, RW_HEAD).transpose(1, 3, 0, 2).reshape(t, RW_HEAD, bh)

            def head_tile(p):
                return jnp.tile(p.reshape(RW_HEADS, RW_HEAD).T, (1, b))

            s0 = wkv0.transpose(3, 2, 0, 1).reshape(RW_HEAD, RW_HEAD, bh)
            y, s_fin = _rwkv_scan(to_scan(r), to_scan(k), to_scan(v), to_scan(w), to_scan(a),
                                  head_tile(wts['rw_k_k']), head_tile(wts['rw_k_a']), head_tile(wts['rw_r_k']),
                                  head_tile(wts['rw_ln_g']), head_tile(wts['rw_ln_b']), s0, min(t, 64))
            y = y.reshape(t, RW_HEAD, b, RW_HEADS).transpose(2, 0, 3, 1).reshape(b, t, d)
            wkv_new = s_fin.reshape(RW_HEAD, RW_HEAD, b, RW_HEADS).transpose(2, 3, 1, 0)
            x = _rwkv_out(y, gg, x, mod, wts['rw_w_o'], tm)
            new_states.append((shift_new, wkv_new))
        x, f_st = _ffn(x, mod, g_norm[i, 1][None, :], wts['ffn_w_up'][i], wts['ffn_conv_w'][i],
                       wts['ffn_conv_b'][i][None, :], wts['ffn_w_down'][i], ffn_left[i],
                       wts['g_final'][None, :], tm, i == DEPTH - 1)
        new_ffn.append(f_st)
    return x, new_states, jnp.stack(new_ffn)


def _pad_cols(w, cols):
    return jnp.pad(w, ((0, 0), (0, cols - w.shape[1])))


def _pad_rows2(w, rows):
    return jnp.pad(w, ((0, rows - w.shape[0]), (0, 0)))


def kernel(x_prompt, x_sample, c_prompt, c_sample, cache_k_0, cache_v_0, cache_kidx_0, state_conv_1, state_shift_2, state_wkv_2, cache_k_3, cache_v_3, cache_kidx_3, state_ffn_conv, w_mod, b_mod, g_norm, g_final, attn_w_in, attn_w_out, sc_w_in, sc_conv_w, sc_w_out, rw_mix, rw_w_rkv, rw_w_o, rw_w0, rw_w1, rw_w2, rw_a0, rw_a1, rw_a2, rw_g1, rw_g2, rw_k_k, rw_k_a, rw_r_k, rw_ln_g, rw_ln_b, ffn_w_up, ffn_conv_w, ffn_conv_b, ffn_w_down):
    d = D_MODEL
    bp = x_prompt.shape[0]
    bs = x_sample.shape[0]
    dt = x_prompt.dtype

    wts = _prep_weights(g_norm, g_final, attn_w_in, attn_w_out, sc_w_in, sc_conv_w, sc_w_out, rw_mix, rw_w_rkv,
                        rw_w_o, rw_w0, rw_w1, rw_w2, rw_a0, rw_a1, rw_a2, rw_g1, rw_g2, rw_k_k, rw_k_a, rw_r_k,
                        rw_ln_g, rw_ln_b, ffn_w_up, ffn_conv_w, ffn_conv_b, ffn_w_down)

    mods = _mod_all(jnp.concatenate([c_prompt, c_sample], axis=0), w_mod, b_mod)
    mods_p = mods[:, :bp].reshape(DEPTH, bp, 6, d)
    mods_s = mods[:, bp:].reshape(DEPTH, bs, 6, d)

    ffn0 = jnp.zeros((DEPTH, bp, 2, 2 * D_FF), dt)
    y_p, st_p, ffn_p = _trunk(x_prompt, mods_p, _empty_states(bp, dt), ffn0, wts, 0)
    sample_states = [(cache_k_0, cache_v_0, cache_kidx_0), (state_conv_1,), (state_shift_2, state_wkv_2),
                     (cache_k_3, cache_v_3, cache_kidx_3)]
    y_s, st_s, ffn_s = _trunk(x_sample, mods_s, sample_states, state_ffn_conv, wts, cache_k_0.shape[1])
    (k0_p, v0_p, ki0_p), (conv1_p,), (shift2_p, wkv2_p), (k3_p, v3_p, ki3_p) = st_p
    (k0_s, v0_s, ki0_s), (conv1_s,), (shift2_s, wkv2_s), (k3_s, v3_s, ki3_s) = st_s
    return (y_p, y_s,
            k0_p, v0_p, ki0_p, conv1_p, shift2_p, wkv2_p, k3_p, v3_p, ki3_p, ffn_p,
            k0_s, v0_s, ki0_s, conv1_s, shift2_s, wkv2_s, k3_s, v3_s, ki3_s, ffn_s)


def _empty_states(b, dt):
    d = D_MODEL
    kv = (jnp.zeros((b, 0, N_KV_HEADS, HEAD_DIM), dt), jnp.zeros((b, 0, N_KV_HEADS, HEAD_DIM), dt),
          jnp.zeros((b, 0, IDX_DIM), dt))
    return [kv, (jnp.zeros((b, 2, d), dt),),
            (jnp.zeros((b, 1, d), dt), jnp.zeros((b, RW_HEADS, RW_HEAD, RW_HEAD), dt)), kv]


def _prep_weights(g_norm, g_final, attn_w_in, attn_w_out, sc_w_in, sc_conv_w, sc_w_out, rw_mix, rw_w_rkv, rw_w_o,
                  rw_w0, rw_w1, rw_w2, rw_a0, rw_a1, rw_a2, rw_g1, rw_g2, rw_k_k, rw_k_a, rw_r_k, rw_ln_g, rw_ln_b,
                  ffn_w_up, ffn_conv_w, ffn_conv_b, ffn_w_down):
    return {
        'g_norm': g_norm, 'g_final': g_final,
        'attn_w_main': attn_w_in[:, :, :QI_END].astype(BF16),
        'attn_w_tail': jnp.pad(attn_w_in[:, :, QI_END:], ((0, 0), (0, 0), (0, LANES - (WI_END - QI_END)))).astype(BF16),
        'attn_w_out': attn_w_out.astype(BF16),
        'sc_w_in': sc_w_in.astype(BF16), 'sc_conv_w': sc_conv_w, 'sc_w_out': sc_w_out.astype(BF16),
        'rw_mix': rw_mix, 'rw_w_rkv': rw_w_rkv.astype(BF16), 'rw_w_o': rw_w_o.astype(BF16),
        'rw_w0': rw_w0[None, :], 'rw_w1': _pad_cols(rw_w1, LANES).astype(BF16),
        'rw_w2': _pad_rows2(rw_w2, LANES).astype(BF16),
        'rw_a0': rw_a0[None, :], 'rw_a1': _pad_cols(rw_a1, LANES).astype(BF16),
        'rw_a2': _pad_rows2(rw_a2, LANES).astype(BF16),
        'rw_g1': rw_g1.astype(BF16), 'rw_g2': rw_g2.astype(BF16),
        'rw_k_k': rw_k_k, 'rw_k_a': rw_k_a, 'rw_r_k': rw_r_k, 'rw_ln_g': rw_ln_g, 'rw_ln_b': rw_ln_b,
        'ffn_w_up': ffn_w_up.astype(BF16), 'ffn_conv_w': ffn_conv_w, 'ffn_conv_b': ffn_conv_b,
        'ffn_w_down': ffn_w_down.astype(BF16),
    }
```

```python
import functools

import jax
import jax.numpy as jnp
from jax import lax
from jax.experimental import pallas as pl
from jax.experimental.pallas import tpu as pltpu

D_MODEL = 1024
DEPTH = 4
CHUNK = 64
N_HEADS = 8
HEAD_DIM = 128
N_KV_HEADS = 2
GROUP = N_HEADS // N_KV_HEADS
IDX_HEADS = 8
IDX_DIM = 64
TOPK_MAX = 256
ROPE_THETA = 10000.0
RW_HEAD = 64
RW_HEADS = D_MODEL // RW_HEAD
RW_GN_EPS = 64e-5
D_FF = 2 * D_MODEL
NORM_EPS = 1e-6

LANES = 128
SUBLANES = 8
VMEM_LIMIT = 52 * 1024 * 1024
INT_MIN = -2 ** 31

F32 = jnp.float32
BF16 = jnp.bfloat16

Q_END = N_HEADS * HEAD_DIM
K_END = Q_END + N_KV_HEADS * HEAD_DIM
V_END = K_END + N_KV_HEADS * HEAD_DIM
QI_END = V_END + IDX_HEADS * IDX_DIM
KI_END = QI_END + IDX_DIM
WI_END = KI_END + IDX_HEADS
Q_SCALE = HEAD_DIM ** -0.5 * 1.4426950408889634


def _cparams(sem):
    return pltpu.CompilerParams(dimension_semantics=sem, vmem_limit_bytes=VMEM_LIMIT)


def _dot(a, b):
    return jnp.dot(a, b, preferred_element_type=F32)


def _dot_nt(a, b):
    return lax.dot_general(a, b, (((1,), (1,)), ((), ())), preferred_element_type=F32)


def _norm_mod(x, g, scale, shift):
    ms = jnp.mean(x * x, axis=-1, keepdims=True)
    return (x * lax.rsqrt(ms + NORM_EPS) * g) * (1.0 + scale) + shift


def _shifted(u, hist, n):
    tm, c = u.shape
    tiles = jnp.concatenate([hist, u], axis=0).reshape(tm // SUBLANES + 1, SUBLANES, c)
    row = lax.broadcasted_iota(jnp.int32, (1, SUBLANES, 1), 1)
    outs = []
    for s in range(n, 0, -1):
        rot = pltpu.roll(tiles, s, 1)
        outs.append(jnp.where(row < s, rot[:-1], rot[1:]).reshape(tm, c))
    return outs


def _causal_conv3(u, hist, w0, w1, w2):
    s2, s1 = _shifted(u, hist, 2)
    return w0 * s2 + w1 * s1 + w2 * u


def _mod_kernel(c_ref, w_ref, b_ref, o_ref):
    o_ref[0] = _dot(c_ref[...].astype(BF16), w_ref[0].astype(BF16)) + b_ref[0]


def _mod_all(c_all, w_mod, b_mod):
    nb = c_all.shape[0]
    d = D_MODEL
    return pl.pallas_call(
        _mod_kernel,
        name="adaln_mod",
        out_shape=jax.ShapeDtypeStruct((DEPTH, nb, 6 * d), F32),
        grid=(DEPTH, 6),
        in_specs=[pl.BlockSpec((nb, d), lambda l, n: (0, 0)),
                  pl.BlockSpec((1, d, d), lambda l, n: (l, 0, n)),
                  pl.BlockSpec((1, 1, d), lambda l, n: (l, 0, n))],
        out_specs=pl.BlockSpec((1, nb, d), lambda l, n: (l, 0, n)),
        compiler_params=_cparams(("parallel", "parallel")),
    )(c_all, w_mod, b_mod.reshape(DEPTH, 1, 6 * d))


def _rope128(seg, cos, sin_signed):
    return seg * cos + pltpu.roll(seg, HEAD_DIM // 2, 1) * sin_signed


def _rope64(seg, cos, sin_lo, sin_hi):
    return seg * cos + pltpu.roll(seg, LANES - IDX_DIM // 2, 1) * sin_lo + pltpu.roll(seg, IDX_DIM // 2, 1) * sin_hi


def _attn_proj_kernel(x_ref, mod_ref, g_ref, wm_ref, wt_ref, c128_ref, s128_ref, c64_ref, s64lo_ref,
                      s64hi_ref, ct_ref, stlo_ref, sthi_ref, q_ref, k_ref, v_ref, qi_ref, kw_ref):
    h = _norm_mod(x_ref[0], g_ref[...], mod_ref[0, 1:2, :], mod_ref[0, 0:1, :]).astype(BF16)
    pm = _dot(h, wm_ref[...])
    c128 = c128_ref[...]
    s128 = s128_ref[...]
    for hh in range(N_HEADS):
        lo = hh * HEAD_DIM
        q_ref[0, :, lo:lo + HEAD_DIM] = _rope128(pm[:, lo:lo + HEAD_DIM], c128, s128) * Q_SCALE
    for hh in range(N_KV_HEADS):
        lo = hh * HEAD_DIM
        k_ref[0, :, lo:lo + HEAD_DIM] = _rope128(pm[:, Q_END + lo:Q_END + lo + HEAD_DIM], c128, s128)
    v_ref[0] = pm[:, K_END:V_END]
    c64 = c64_ref[...]
    s64lo = s64lo_ref[...]
    s64hi = s64hi_ref[...]
    for cc in range(IDX_HEADS * IDX_DIM // LANES):
        lo = cc * LANES
        qi_ref[0, :, lo:lo + LANES] = _rope64(pm[:, V_END + lo:V_END + lo + LANES], c64, s64lo, s64hi)
    pt = _dot(h, wt_ref[...])
    kw_ref[0] = _rope64(pt, ct_ref[...], stlo_ref[...], sthi_ref[...])


def _rope_tables(t_len, past):
    pos = jnp.arange(past, past + t_len, dtype=jnp.int32).astype(F32)[:, None]

    def cs(d):
        half = d // 2
        inv = ROPE_THETA ** (-2.0 * jnp.arange(half, dtype=F32) / d)
        ang = pos * inv[None, :]
        return jnp.cos(ang), jnp.sin(ang)

    c, s = cs(HEAD_DIM)
    c128 = jnp.concatenate([c, c], axis=1)
    s128 = jnp.concatenate([-s, s], axis=1)
    c, s = cs(IDX_DIM)
    z = jnp.zeros_like(s)
    c64 = jnp.concatenate([c, c, c, c], axis=1)
    s64lo = jnp.concatenate([-s, z, -s, z], axis=1)
    s64hi = jnp.concatenate([z, s, z, s], axis=1)
    wscale = jnp.full((t_len, LANES - IDX_DIM), IDX_HEADS ** -0.5, F32)
    ct = jnp.concatenate([c, c, wscale], axis=1)
    stlo = jnp.concatenate([-s, z, z, z], axis=1)
    sthi = jnp.concatenate([z, s, z, z], axis=1)
    return c128, s128, c64, s64lo, s64hi, ct, stlo, sthi


def _attn_proj(x, mod, g, w_main, w_tail, tables, tm):
    b, t, d = x.shape
    tab_spec = pl.BlockSpec((tm, LANES), lambda bi, i: (i, 0))
    const2 = lambda bi, i: (0, 0)
    row3 = lambda bi, i: (bi, i, 0)
    return pl.pallas_call(
        _attn_proj_kernel,
        name="attn_proj",
        out_shape=(jax.ShapeDtypeStruct((b, t, Q_END), F32),
                   jax.ShapeDtypeStruct((b, t, K_END - Q_END), F32),
                   jax.ShapeDtypeStruct((b, t, V_END - K_END), F32),
                   jax.ShapeDtypeStruct((b, t, QI_END - V_END), F32),
                   jax.ShapeDtypeStruct((b, t, LANES), F32)),
        grid=(b, t // tm),
        in_specs=[pl.BlockSpec((1, tm, d), row3),
                  pl.BlockSpec((1, 6, d), lambda bi, i: (bi, 0, 0)),
                  pl.BlockSpec((1, d), const2),
                  pl.BlockSpec((d, QI_END), const2),
                  pl.BlockSpec((d, LANES), const2)] + [tab_spec] * 8,
        out_specs=(pl.BlockSpec((1, tm, Q_END), row3),
                   pl.BlockSpec((1, tm, K_END - Q_END), row3),
                   pl.BlockSpec((1, tm, V_END - K_END), row3),
                   pl.BlockSpec((1, tm, QI_END - V_END), row3),
                   pl.BlockSpec((1, tm, LANES), row3)),
        compiler_params=_cparams(("parallel", "parallel")),
    )(x, mod, g, w_main, w_tail, *tables)


def _attn_core_kernel(q_ref, qi_ref, kw_ref, k_ref, v_ref, ki_ref, x_ref, mod_ref, wo_ref, o_ref,
                      *, pos0, l_true, n_sel):
    tq = q_ref.shape[1]
    lp = k_ref.shape[1]
    i = pl.program_id(1)
    qpos = pos0 + i * tq + lax.broadcasted_iota(jnp.int32, (tq, 1), 0)
    kidx = lax.broadcasted_iota(jnp.int32, (tq, lp), 1)
    adm = (lax.shift_right_logical(kidx, 6) <= lax.shift_right_logical(qpos, 6)) & (kidx < l_true)

    kib = ki_ref[0].astype(BF16)
    score = jnp.zeros((tq, lp), F32)
    for hh in range(IDX_HEADS):
        qh = qi_ref[0, :, hh * IDX_DIM:(hh + 1) * IDX_DIM].astype(BF16)
        lg = _dot_nt(qh, kib)
        wi = kw_ref[0, :, IDX_DIM + hh:IDX_DIM + hh + 1]
        score = score + wi * jnp.maximum(lg, 0.0)

    bits = lax.bitcast_convert_type(score + 0.0, jnp.int32)
    key = bits ^ ((bits >> 31) & jnp.int32(0x7FFFFFFF))
    key = jnp.where(adm, key, jnp.int32(INT_MIN))
    kf = jnp.float32(n_sel)

    def thr_body(it, t):
        cand = t + lax.shift_left(jnp.int32(1), 31 - it)
        cnt = jnp.sum(jnp.where(key >= cand, 1.0, 0.0), axis=1, keepdims=True)
        return jnp.where(cnt >= kf, cand, t)

    thr = lax.fori_loop(0, 32, thr_body, jnp.full((tq, 1), INT_MIN, jnp.int32))
    gt = key > thr
    eq = (key == thr) & adm
    need = kf - jnp.sum(jnp.where(gt, 1.0, 0.0), axis=1, keepdims=True)
    eqf = jnp.where(eq, 1.0, 0.0)
    excess = jnp.sum(eqf, axis=1, keepdims=True) - need

    def tie_search():
        nbits = lp.bit_length()

        def body(it, m):
            cand = m + lax.shift_left(jnp.int32(1), nbits - 1 - it)
            f = jnp.sum(jnp.where(kidx < cand, eqf, 0.0), axis=1, keepdims=True)
            return jnp.where(f < need, cand, m)

        return lax.fori_loop(0, nbits, body, jnp.zeros((tq, 1), jnp.int32)) + 1

    jstar = lax.cond(jnp.max(excess) > 0.0, tie_search, lambda: jnp.full((tq, 1), lp, jnp.int32))
    sel = gt | (eq & (kidx < jstar))

    outs = []
    for kv in range(N_KV_HEADS):
        kk = k_ref[0, :, kv * HEAD_DIM:(kv + 1) * HEAD_DIM].astype(BF16)
        vv = v_ref[0, :, kv * HEAD_DIM:(kv + 1) * HEAD_DIM].astype(BF16)
        for gi in range(GROUP):
            hq = kv * GROUP + gi
            qh = q_ref[0, :, hq * HEAD_DIM:(hq + 1) * HEAD_DIM].astype(BF16)
            s = jnp.where(sel, _dot_nt(qh, kk), -jnp.inf)
            m = jnp.max(s, axis=1, keepdims=True)
            p = jnp.exp2(s - m)
            l = jnp.sum(p, axis=1, keepdims=True)
            outs.append(_dot(p.astype(BF16), vv) / l)
    o = jnp.concatenate(outs, axis=1).astype(BF16)
    o_ref[0] = x_ref[0] + mod_ref[0, 2:3, :] * _dot(o, wo_ref[...])


def _attn_core(q, qi, kw, k_all, v_all, ki_all, x, mod, w_out, *, past, row0, rows, n_sel, tq):
    b, t, d = x.shape
    l_true = past + row0 + rows
    lp = -(-l_true // LANES) * LANES
    blk0 = row0 // tq
    row3 = lambda bi, i: (bi, blk0 + i, 0)
    per_b = lambda bi, i: (bi, 0, 0)
    kern = functools.partial(_attn_core_kernel, pos0=past + row0, l_true=l_true, n_sel=n_sel)
    return pl.pallas_call(
        kern,
        name=f"attn_core_l{lp}",
        out_shape=jax.ShapeDtypeStruct((b, t, d), F32),
        input_output_aliases={6: 0},
        grid=(b, rows // tq),
        in_specs=[pl.BlockSpec((1, tq, Q_END), row3),
                  pl.BlockSpec((1, tq, QI_END - V_END), row3),
                  pl.BlockSpec((1, tq, LANES), row3),
                  pl.BlockSpec((1, lp, K_END - Q_END), per_b),
                  pl.BlockSpec((1, lp, V_END - K_END), per_b),
                  pl.BlockSpec((1, lp, IDX_DIM), per_b),
                  pl.BlockSpec((1, tq, d), row3),
                  pl.BlockSpec((1, 6, d), per_b),
                  pl.BlockSpec((N_HEADS * HEAD_DIM, d), lambda bi, i: (0, 0))],
        out_specs=pl.BlockSpec((1, tq, d), row3),
        compiler_params=_cparams(("parallel", "parallel")),
    )(q, qi, kw, k_all, v_all, ki_all, x, mod, w_out)


def _conv_mixer_kernel(x_ref, mod_ref, g_ref, win_ref, cw_ref, wout_ref, left_ref, o_ref, st_ref, carry_ref):
    d = D_MODEL
    tm = x_ref.shape[1]

    @pl.when(pl.program_id(1) == 0)
    def _():
        carry_ref[...] = jnp.zeros_like(carry_ref)
        carry_ref[6:8, :] = left_ref[0]

    x = x_ref[0]
    h = _norm_mod(x, g_ref[...], mod_ref[0, 1:2, :], mod_ref[0, 0:1, :]).astype(BF16)
    gb = _dot(h, win_ref[:, 0:d])
    gc = _dot(h, win_ref[:, d:2 * d])
    u = gc * _dot(h, win_ref[:, 2 * d:3 * d])
    y = _causal_conv3(u, carry_ref[...], cw_ref[0:1, :], cw_ref[1:2, :], cw_ref[2:3, :])
    carry_ref[...] = u[tm - SUBLANES:tm, :]
    st_ref[0] = u[tm - 2:tm, :]
    o_ref[0] = x + mod_ref[0, 2:3, :] * _dot((gb * y).astype(BF16), wout_ref[...])


def _conv_mixer(x, mod, g, w_in, conv_w, w_out, left, tm):
    b, t, d = x.shape
    row3 = lambda bi, i: (bi, i, 0)
    per_b = lambda bi, i: (bi, 0, 0)
    const2 = lambda bi, i: (0, 0)
    return pl.pallas_call(
        _conv_mixer_kernel,
        name="conv_mixer",
        out_shape=(jax.ShapeDtypeStruct((b, t, d), F32), jax.ShapeDtypeStruct((b, 2, d), F32)),
        grid=(b, t // tm),
        in_specs=[pl.BlockSpec((1, tm, d), row3),
                  pl.BlockSpec((1, 6, d), per_b),
                  pl.BlockSpec((1, d), const2),
                  pl.BlockSpec((d, 3 * d), const2),
                  pl.BlockSpec((3, d), const2),
                  pl.BlockSpec((d, d), const2),
                  pl.BlockSpec((1, 2, d), per_b)],
        out_specs=(pl.BlockSpec((1, tm, d), row3), pl.BlockSpec((1, 2, d), per_b)),
        scratch_shapes=[pltpu.VMEM((SUBLANES, d), F32)],
        compiler_params=_cparams(("parallel", "arbitrary")),
    )(x, mod, g, w_in, conv_w, w_out, left)


FFN_COLS = 512


def _ffn_kernel(x_ref, mod_ref, g_ref, wup_ref, cw_ref, cb_ref, wdn_ref, left_ref, gfin_ref, o_ref, st_ref,
                carry_ref, *, final_norm):
    tm = x_ref.shape[1]

    @pl.when(pl.program_id(1) == 0)
    def _():
        carry_ref[...] = jnp.zeros_like(carry_ref)
        carry_ref[6:8, :] = left_ref[0]

    x = x_ref[0]
    h = _norm_mod(x, g_ref[...], mod_ref[0, 4:5, :], mod_ref[0, 3:4, :]).astype(BF16)

    def conv_cols(lo):
        u = _dot(h, wup_ref[:, lo:lo + FFN_COLS])
        z = _causal_conv3(u, carry_ref[:, lo:lo + FFN_COLS], cw_ref[0:1, lo:lo + FFN_COLS],
                          cw_ref[1:2, lo:lo + FFN_COLS], cw_ref[2:3, lo:lo + FFN_COLS]) + cb_ref[:, lo:lo + FFN_COLS]
        carry_ref[:, lo:lo + FFN_COLS] = u[tm - SUBLANES:tm, :]
        st_ref[0, :, lo:lo + FFN_COLS] = u[tm - 2:tm, :]
        return z

    acc = jnp.zeros((tm, D_MODEL), F32)
    for c in range(0, D_FF, FFN_COLS):
        gate = conv_cols(c)
        val = conv_cols(D_FF + c)
        act = (gate * jax.nn.sigmoid(gate)) * val
        acc = acc + _dot(act.astype(BF16), wdn_ref[c:c + FFN_COLS, :])
    y = x + mod_ref[0, 5:6, :] * acc
    if final_norm:
        ms = jnp.mean(y * y, axis=-1, keepdims=True)
        y = y * lax.rsqrt(ms + NORM_EPS) * gfin_ref[...]
    o_ref[0] = y


def _ffn(x, mod, g, w_up, conv_w, conv_b, w_down, left, g_final, tm, final_norm):
    b, t, d = x.shape
    row3 = lambda bi, i: (bi, i, 0)
    per_b = lambda bi, i: (bi, 0, 0)
    const2 = lambda bi, i: (0, 0)
    kern = functools.partial(_ffn_kernel, final_norm=final_norm)
    return pl.pallas_call(
        kern,
        name="conv_ffn",
        out_shape=(jax.ShapeDtypeStruct((b, t, d), F32), jax.ShapeDtypeStruct((b, 2, 2 * D_FF), F32)),
        grid=(b, t // tm),
        in_specs=[pl.BlockSpec((1, tm, d), row3),
                  pl.BlockSpec((1, 6, d), per_b),
                  pl.BlockSpec((1, d), const2),
                  pl.BlockSpec((d, 2 * D_FF), const2),
                  pl.BlockSpec((3, 2 * D_FF), const2),
                  pl.BlockSpec((1, 2 * D_FF), const2),
                  pl.BlockSpec((D_FF, d), const2),
                  pl.BlockSpec((1, 2, 2 * D_FF), per_b),
                  pl.BlockSpec((1, d), const2)],
        out_specs=(pl.BlockSpec((1, tm, d), row3), pl.BlockSpec((1, 2, 2 * D_FF), per_b)),
        scratch_shapes=[pltpu.VMEM((SUBLANES, 2 * D_FF), F32)],
        compiler_params=_cparams(("parallel", "arbitrary")),
    )(x, mod, g, w_up, conv_w, conv_b, w_down, left, g_final)


def _softplus(z):
    return jnp.maximum(z, 0.0) + jnp.log1p(jnp.exp(-jnp.abs(z)))


def _rwkv_proj_kernel(x_ref, mod_ref, g_ref, sp_ref, mix_ref, wrkv_ref, w1_ref, w2_ref, w0_ref, a1_ref, a2_ref,
                      a0_ref, g1_ref, g2_ref, r_ref, k_ref, v_ref, w_ref, a_ref, gg_ref, sh_ref, carry_ref):
    tm = x_ref.shape[1]

    @pl.when(pl.program_id(1) == 0)
    def _():
        carry_ref[...] = jnp.zeros_like(carry_ref)
        carry_ref[7:8, :] = sp_ref[0]

    h = _norm_mod(x_ref[0], g_ref[...], mod_ref[0, 1:2, :], mod_ref[0, 0:1, :])
    (hs,) = _shifted(h, carry_ref[...], 1)
    carry_ref[...] = h[tm - SUBLANES:tm, :]
    sh_ref[0] = h[tm - 1:tm, :]
    xx = hs - h

    def mixed(j):
        return (h + xx * mix_ref[j:j + 1, :]).astype(BF16)

    r_ref[0] = _dot(mixed(0), wrkv_ref[0])
    k_ref[0] = _dot(mixed(2), wrkv_ref[1])
    v_ref[0] = _dot(mixed(3), wrkv_ref[2])
    lw = _dot(jnp.tanh(_dot(mixed(1), w1_ref[...])).astype(BF16), w2_ref[...])
    w_log = -_softplus(-(w0_ref[...] + lw)) - 0.5
    w_ref[0] = jnp.exp(-jnp.exp(w_log))
    a_ref[0] = jax.nn.sigmoid(a0_ref[...] + _dot(_dot(mixed(4), a1_ref[...]).astype(BF16), a2_ref[...]))
    gg_ref[0] = _dot(jax.nn.sigmoid(_dot(mixed(5), g1_ref[...])).astype(BF16), g2_ref[...])


def _rwkv_proj(x, mod, g, shift_prev, mix, w_rkv, w1, w2, w0, a1, a2, a0, g1, g2, tm):
    b, t, d = x.shape
    row3 = lambda bi, i: (bi, i, 0)
    per_b = lambda bi, i: (bi, 0, 0)
    const2 = lambda bi, i: (0, 0)
    big = jax.ShapeDtypeStruct((b, t, d), F32)
    lora = w1.shape[1]
    gl = g1.shape[1]
    return pl.pallas_call(
        _rwkv_proj_kernel,
        name="rwkv_proj",
        out_shape=(big,) * 6 + (jax.ShapeDtypeStruct((b, 1, d), F32),),
        grid=(b, t // tm),
        in_specs=[pl.BlockSpec((1, tm, d), row3),
                  pl.BlockSpec((1, 6, d), per_b),
                  pl.BlockSpec((1, d), const2),
                  pl.BlockSpec((1, 1, d), per_b),
                  pl.BlockSpec((6, d), const2),
                  pl.BlockSpec((3, d, d), lambda bi, i: (0, 0, 0)),
                  pl.BlockSpec((d, lora), const2),
                  pl.BlockSpec((lora, d), const2),
                  pl.BlockSpec((1, d), const2),
                  pl.BlockSpec((d, lora), const2),
                  pl.BlockSpec((lora, d), const2),
                  pl.BlockSpec((1, d), const2),
                  pl.BlockSpec((d, gl), const2),
                  pl.BlockSpec((gl, d), const2)],
        out_specs=(pl.BlockSpec((1, tm, d), row3),) * 6 + (pl.BlockSpec((1, 1, d), per_b),),
        scratch_shapes=[pltpu.VMEM((SUBLANES, d), F32)],
        compiler_params=_cparams(("parallel", "arbitrary")),
    )(x, mod, g, shift_prev, mix, w_rkv, w1, w2, w0, a1, a2, a0, g1, g2)


def _rwkv_scan_kernel(r_ref, k_ref, v_ref, w_ref, a_ref, kk_ref, ka_ref, rk_ref, lg_ref, lb_ref, s0_ref,
                      y_ref, sfin_ref, s_ref, bc_ref):
    tc = r_ref.shape[0]
    n = RW_HEAD

    @pl.when(pl.program_id(1) == 0)
    def _():
        s_ref[...] = s0_ref[...]

    def step(t, carry):
        kt = k_ref[t]
        at = a_ref[t]
        rt = r_ref[t]
        vt = v_ref[t]
        kkr = kt * kk_ref[...]
        nrm = jnp.sqrt(jnp.sum(kkr * kkr, axis=0, keepdims=True))
        kk = kkr / jnp.maximum(nrm, 1e-12)
        kmod = kt * (1.0 + (at - 1.0) * ka_ref[...])
        bc_ref[0] = -kk
        bc_ref[1] = w_ref[t]
        bc_ref[2] = kk * at
        bc_ref[3] = kmod
        bc_ref[4] = rt
        sa = jnp.zeros((n, LANES), F32)
        for j in range(n):
            sa = sa + s_ref[j] * bc_ref[0, j:j + 1, :]
        y = jnp.zeros((n, LANES), F32)
        for j in range(n):
            sj = s_ref[j] * bc_ref[1, j:j + 1, :] + sa * bc_ref[2, j:j + 1, :] + vt * bc_ref[3, j:j + 1, :]
            s_ref[j] = sj
            y = y + sj * bc_ref[4, j:j + 1, :]
        mu = jnp.mean(y, axis=0, keepdims=True)
        yc = y - mu
        var = jnp.mean(yc * yc, axis=0, keepdims=True)
        yn = (yc * lax.rsqrt(var + RW_GN_EPS)) * lg_ref[...] + lb_ref[...]
        bonus = jnp.sum(rt * kmod * rk_ref[...], axis=0, keepdims=True) * vt
        y_ref[t] = yn + bonus
        return carry

    lax.fori_loop(0, tc, step, 0)
    sfin_ref[...] = s_ref[...]


def _rwkv_scan(r, k, v, w, a, kk_t, ka_t, rk_t, lg_t, lb_t, s0, tc):
    t, n, bh = r.shape
    seq = pl.BlockSpec((tc, n, LANES), lambda lb, ti: (ti, 0, lb))
    par = pl.BlockSpec((n, LANES), lambda lb, ti: (0, lb))
    st = pl.BlockSpec((n, n, LANES), lambda lb, ti: (0, 0, lb))
    return pl.pallas_call(
        _rwkv_scan_kernel,
        name="rwkv_scan",
        out_shape=(jax.ShapeDtypeStruct((t, n, bh), F32), jax.ShapeDtypeStruct((n, n, bh), F32)),
        grid=(bh // LANES, t // tc),
        in_specs=[seq] * 5 + [par] * 5 + [st],
        out_specs=(seq, st),
        scratch_shapes=[pltpu.VMEM((n, n, LANES), F32), pltpu.VMEM((5, n, LANES), F32)],
        compiler_params=_cparams(("parallel", "arbitrary")),
    )(r, k, v, w, a, kk_t, ka_t, rk_t, lg_t, lb_t, s0)


def _rwkv_out_kernel(y_ref, gg_ref, x_ref, mod_ref, wo_ref, o_ref):
    o_ref[0] = x_ref[0] + mod_ref[0, 2:3, :] * _dot((y_ref[0] * gg_ref[0]).astype(BF16), wo_ref[...])


def _rwkv_out(y, gg, x, mod, w_o, tm):
    b, t, d = x.shape
    row3 = lambda bi, i: (bi, i, 0)
    return pl.pallas_call(
        _rwkv_out_kernel,
        name="rwkv_out",
        out_shape=jax.ShapeDtypeStruct((b, t, d), F32),
        grid=(b, t // tm),
        in_specs=[pl.BlockSpec((1, tm, d), row3), pl.BlockSpec((1, tm, d), row3), pl.BlockSpec((1, tm, d), row3),
                  pl.BlockSpec((1, 6, d), lambda bi, i: (bi, 0, 0)),
                  pl.BlockSpec((d, d), lambda bi, i: (0, 0))],
        out_specs=pl.BlockSpec((1, tm, d), row3),
        compiler_params=_cparams(("parallel", "parallel")),
    )(y, gg, x, mod, w_o)


def _pad_rows(a, rows):
    return jnp.pad(a, ((0, 0), (0, rows - a.shape[1]), (0, 0)))


def _trunk(x, mods, states, ffn_left, wts, past):
    b, t, d = x.shape
    tm = min(t, 256)
    tq = min(t, 256)
    g_norm = wts['g_norm']
    new_states = []
    new_ffn = []
    tables = _rope_tables(t, past)
    for i in range(DEPTH):
        mod = mods[i]
        kind = i % 3
        g_a = g_norm[i, 0][None, :]
        if kind == 0:
            j = i // 3
            past_k, past_v, past_ki = states[i]
            q, k, v, qi, kw = _attn_proj(x, mod, g_a, wts['attn_w_main'][j], wts['attn_w_tail'][j], tables, tm)
            ki = kw[:, :, :IDX_DIM]
            l_all = past + t
            if past:
                lp = -(-l_all // LANES) * LANES
                kvd = N_KV_HEADS * HEAD_DIM
                k_all = _pad_rows(jnp.concatenate([past_k.reshape(b, past, kvd), k], axis=1), lp)
                v_all = _pad_rows(jnp.concatenate([past_v.reshape(b, past, kvd), v], axis=1), lp)
                ki_all = _pad_rows(jnp.concatenate([past_ki, ki], axis=1), lp)
            else:
                k_all, v_all, ki_all = k, v, ki
            for row0 in range(0, t, tq):
                x = _attn_core(q, qi, kw, k_all, v_all, ki_all, x, mod, wts['attn_w_out'][j], past=past,
                               row0=row0, rows=tq, n_sel=min(TOPK_MAX, l_all // 4), tq=tq)
            new_states.append((k.reshape(b, t, N_KV_HEADS, HEAD_DIM), v.reshape(b, t, N_KV_HEADS, HEAD_DIM), ki))
        elif kind == 1:
            x, conv_st = _conv_mixer(x, mod, g_a, wts['sc_w_in'], wts['sc_conv_w'], wts['sc_w_out'],
                                     states[i][0], tm)
            new_states.append((conv_st,))
        else:
            shift_prev, wkv0 = states[i]
            r, k, v, w, a, gg, shift_new = _rwkv_proj(
                x, mod, g_a, shift_prev, wts['rw_mix'], wts['rw_w_rkv'], wts['rw_w1'], wts['rw_w2'], wts['rw_w0'],
                wts['rw_a1'], wts['rw_a2'], wts['rw_a0'], wts['rw_g1'], wts['rw_g2'], tm)
            bh = b * RW_HEADS

            def to_scan(z):
                return z.reshape(b, t, RW_HEADS, RW_HEAD).transpose(1, 3, 0, 2).reshape(t, RW_HEAD, bh)

            def head_tile(p):
                return jnp.tile(p.reshape(RW_HEADS, RW_HEAD).T, (1, b))

            s0 = wkv0.transpose(3, 2, 0, 1).reshape(RW_HEAD, RW_HEAD, bh)
            y, s_fin = _rwkv_scan(to_scan(r), to_scan(k), to_scan(v), to_scan(w), to_scan(a),
                                  head_tile(wts['rw_k_k']), head_tile(wts['rw_k_a']), head_tile(wts['rw_r_k']),
                                  head_tile(wts['rw_ln_g']), head_tile(wts['rw_ln_b']), s0, min(t, 64))
            y = y.reshape(t, RW_HEAD, b, RW_HEADS).transpose(2, 0, 3, 1).reshape(b, t, d)
            wkv_new = s_fin.reshape(RW_HEAD, RW_HEAD, b, RW_HEADS).transpose(2, 3, 1, 0)
            x = _rwkv_out(y, gg, x, mod, wts['rw_w_o'], tm)
            new_states.append((shift_new, wkv_new))
        x, f_st = _ffn(x, mod, g_norm[i, 1][None, :], wts['ffn_w_up'][i], wts['ffn_conv_w'][i],
                       wts['ffn_conv_b'][i][None, :], wts['ffn_w_down'][i], ffn_left[i],
                       wts['g_final'][None, :], tm, i == DEPTH - 1)
        new_ffn.append(f_st)
    return x, new_states, jnp.stack(new_ffn)


def _pad_cols(w, cols):
    return jnp.pad(w, ((0, 0), (0, cols - w.shape[1])))


def _pad_rows2(w, rows):
    return jnp.pad(w, ((0, rows - w.shape[0]), (0, 0)))


def kernel(x_prompt, x_sample, c_prompt, c_sample, cache_k_0, cache_v_0, cache_kidx_0, state_conv_1, state_shift_2, state_wkv_2, cache_k_3, cache_v_3, cache_kidx_3, state_ffn_conv, w_mod, b_mod, g_norm, g_final, attn_w_in, attn_w_out, sc_w_in, sc_conv_w, sc_w_out, rw_mix, rw_w_rkv, rw_w_o, rw_w0, rw_w1, rw_w2, rw_a0, rw_a1, rw_a2, rw_g1, rw_g2, rw_k_k, rw_k_a, rw_r_k, rw_ln_g, rw_ln_b, ffn_w_up, ffn_conv_w, ffn_conv_b, ffn_w_down):
    d = D_MODEL
    bp = x_prompt.shape[0]
    bs = x_sample.shape[0]
    dt = x_prompt.dtype

    wts = _prep_weights(g_norm, g_final, attn_w_in, attn_w_out, sc_w_in, sc_conv_w, sc_w_out, rw_mix, rw_w_rkv,
                        rw_w_o, rw_w0, rw_w1, rw_w2, rw_a0, rw_a1, rw_a2, rw_g1, rw_g2, rw_k_k, rw_k_a, rw_r_k,
                        rw_ln_g, rw_ln_b, ffn_w_up, ffn_conv_w, ffn_conv_b, ffn_w_down)

    mods = _mod_all(jnp.concatenate([c_prompt, c_sample], axis=0), w_mod, b_mod)
    mods_p = mods[:, :bp].reshape(DEPTH, bp, 6, d)
    mods_s = mods[:, bp:].reshape(DEPTH, bs, 6, d)

    ffn0 = jnp.zeros((DEPTH, bp, 2, 2 * D_FF), dt)
    y_p, st_p, ffn_p = _trunk(x_prompt, mods_p, _empty_states(bp, dt), ffn0, wts, 0)
    sample_states = [(cache_k_0, cache_v_0, cache_kidx_0), (state_conv_1,), (state_shift_2, state_wkv_2),
                     (cache_k_3, cache_v_3, cache_kidx_3)]
    y_s, st_s, ffn_s = _trunk(x_sample, mods_s, sample_states, state_ffn_conv, wts, cache_k_0.shape[1])
    (k0_p, v0_p, ki0_p), (conv1_p,), (shift2_p, wkv2_p), (k3_p, v3_p, ki3_p) = st_p
    (k0_s, v0_s, ki0_s), (conv1_s,), (shift2_s, wkv2_s), (k3_s, v3_s, ki3_s) = st_s
    return (y_p, y_s,
            k0_p, v0_p, ki0_p, conv1_p, shift2_p, wkv2_p, k3_p, v3_p, ki3_p, ffn_p,
            k0_s, v0_s, ki0_s, conv1_s, shift2_s, wkv2_s, k3_s, v3_s, ki3_s, ffn_s)


def _empty_states(b, dt):
    d = D_MODEL
    kv = (jnp.zeros((b, 0, N_KV_HEADS, HEAD_DIM), dt), jnp.zeros((b, 0, N_KV_HEADS, HEAD_DIM), dt),
          jnp.zeros((b, 0, IDX_DIM), dt))
    return [kv, (jnp.zeros((b, 2, d), dt),),
            (jnp.zeros((b, 1, d), dt), jnp.zeros((b, RW_HEADS, RW_HEAD, RW_HEAD), dt)), kv]


def _prep_weights(g_norm, g_final, attn_w_in, attn_w_out, sc_w_in, sc_conv_w, sc_w_out, rw_mix, rw_w_rkv, rw_w_o,
                  rw_w0, rw_w1, rw_w2, rw_a0, rw_a1, rw_a2, rw_g1, rw_g2, rw_k_k, rw_k_a, rw_r_k, rw_ln_g, rw_ln_b,
                  ffn_w_up, ffn_conv_w, ffn_conv_b, ffn_w_down):
    return {
        'g_norm': g_norm, 'g_final': g_final,
        'attn_w_main': attn_w_in[:, :, :QI_END].astype(BF16),
        'attn_w_tail': jnp.pad(attn_w_in[:, :, QI_END:], ((0, 0), (0, 0), (0, LANES - (WI_END - QI_END)))).astype(BF16),
        'attn_w_out': attn_w_out.astype(BF16),
        'sc_w_in': sc_w_in.astype(BF16), 'sc_conv_w': sc_conv_w, 'sc_w_out': sc_w_out.astype(BF16),
        'rw_mix': rw_mix, 'rw_w_rkv': rw_w_rkv.astype(BF16), 'rw_w_o': rw_w_o.astype(BF16),
        'rw_w0': rw_w0[None, :], 'rw_w1': _pad_cols(rw_w1, LANES).astype(BF16),
        'rw_w2': _pad_rows2(rw_w2, LANES).astype(BF16),
        'rw_a0': rw_a0[None, :], 'rw_a1': _pad_cols(rw_a1, LANES).astype(BF16),
        'rw_a2': _pad_rows2(rw_a2, LANES).astype(BF16),
        'rw_g1': rw_g1.astype(BF16), 'rw_g2': rw_g2.astype(BF16),
        'rw_k_k': rw_k_k, 'rw_k_a': rw_k_a, 'rw_r_k': rw_r_k, 'rw_ln_g': rw_ln_g, 'rw_ln_b': rw_ln_b,
        'ffn_w_up': ffn_w_up.astype(BF16), 'ffn_conv_w': ffn_conv_w, 'ffn_conv_b': ffn_conv_b,
        'ffn_w_down': ffn_w_down.astype(BF16),
    }
```

```python
import functools

import jax
import jax.numpy as jnp
from jax import lax
from jax.experimental import pallas as pl
from jax.experimental.pallas import tpu as pltpu

D_MODEL = 1024
DEPTH = 4
CHUNK = 64
N_HEADS = 8
HEAD_DIM = 128
N_KV_HEADS = 2
GROUP = N_HEADS // N_KV_HEADS
IDX_HEADS = 8
IDX_DIM = 64
TOPK_MAX = 256
ROPE_THETA = 10000.0
RW_HEAD = 64
RW_HEADS = D_MODEL // RW_HEAD
RW_GN_EPS = 64e-5
D_FF = 2 * D_MODEL
NORM_EPS = 1e-6

LANES = 128
SUBLANES = 8
VMEM_LIMIT = 52 * 1024 * 1024
INT_MIN = -2 ** 31

F32 = jnp.float32
BF16 = jnp.bfloat16

Q_END = N_HEADS * HEAD_DIM
K_END = Q_END + N_KV_HEADS * HEAD_DIM
V_END = K_END + N_KV_HEADS * HEAD_DIM
QI_END = V_END + IDX_HEADS * IDX_DIM
KI_END = QI_END + IDX_DIM
WI_END = KI_END + IDX_HEADS
Q_SCALE = HEAD_DIM ** -0.5 * 1.4426950408889634


def _cparams(sem):
    return pltpu.CompilerParams(dimension_semantics=sem, vmem_limit_bytes=VMEM_LIMIT)


def _dot(a, b):
    return jnp.dot(a, b, preferred_element_type=F32)


def _dot_nt(a, b):
    return lax.dot_general(a, b, (((1,), (1,)), ((), ())), preferred_element_type=F32)


def _norm_mod(x, g, scale, shift):
    ms = jnp.mean(x * x, axis=-1, keepdims=True)
    return (x * lax.rsqrt(ms + NORM_EPS) * g) * (1.0 + scale) + shift


def _shifted(u, hist, n):
    tm, c = u.shape
    tiles = jnp.concatenate([hist, u], axis=0).reshape(tm // SUBLANES + 1, SUBLANES, c)
    row = lax.broadcasted_iota(jnp.int32, (1, SUBLANES, 1), 1)
    outs = []
    for s in range(n, 0, -1):
        rot = pltpu.roll(tiles, s, 1)
        outs.append(jnp.where(row < s, rot[:-1], rot[1:]).reshape(tm, c))
    return outs


def _causal_conv3(u, hist, w0, w1, w2):
    s2, s1 = _shifted(u, hist, 2)
    return w0 * s2 + w1 * s1 + w2 * u


def _mod_kernel(c_ref, w_ref, b_ref, o_ref):
    o_ref[0] = _dot(c_ref[...].astype(BF16), w_ref[0].astype(BF16)) + b_ref[0]


def _mod_all(c_all, w_mod, b_mod):
    nb = c_all.shape[0]
    d = D_MODEL
    return pl.pallas_call(
        _mod_kernel,
        name="adaln_mod",
        out_shape=jax.ShapeDtypeStruct((DEPTH, nb, 6 * d), F32),
        grid=(DEPTH, 6),
        in_specs=[pl.BlockSpec((nb, d), lambda l, n: (0, 0)),
                  pl.BlockSpec((1, d, d), lambda l, n: (l, 0, n)),
                  pl.BlockSpec((1, 1, d), lambda l, n: (l, 0, n))],
        out_specs=pl.BlockSpec((1, nb, d), lambda l, n: (l, 0, n)),
        compiler_params=_cparams(("parallel", "parallel")),
    )(c_all, w_mod, b_mod.reshape(DEPTH, 1, 6 * d))


def _rope128(seg, cos, sin_signed):
    return seg * cos + pltpu.roll(seg, HEAD_DIM // 2, 1) * sin_signed


def _rope64(seg, cos, sin_lo, sin_hi):
    return seg * cos + pltpu.roll(seg, LANES - IDX_DIM // 2, 1) * sin_lo + pltpu.roll(seg, IDX_DIM // 2, 1) * sin_hi


def _attn_proj_kernel(x_ref, mod_ref, g_ref, wm_ref, wt_ref, c128_ref, s128_ref, c64_ref, s64lo_ref,
                      s64hi_ref, ct_ref, stlo_ref, sthi_ref, q_ref, k_ref, v_ref, qi_ref, kw_ref):
    h = _norm_mod(x_ref[0], g_ref[...], mod_ref[0, 1:2, :], mod_ref[0, 0:1, :]).astype(BF16)
    pm = _dot(h, wm_ref[...])
    c128 = c128_ref[...]
    s128 = s128_ref[...]
    for hh in range(N_HEADS):
        lo = hh * HEAD_DIM
        q_ref[0, :, lo:lo + HEAD_DIM] = _rope128(pm[:, lo:lo + HEAD_DIM], c128, s128) * Q_SCALE
    for hh in range(N_KV_HEADS):
        lo = hh * HEAD_DIM
        k_ref[0, :, lo:lo + HEAD_DIM] = _rope128(pm[:, Q_END + lo:Q_END + lo + HEAD_DIM], c128, s128)
    v_ref[0] = pm[:, K_END:V_END]
    c64 = c64_ref[...]
    s64lo = s64lo_ref[...]
    s64hi = s64hi_ref[...]
    for cc in range(IDX_HEADS * IDX_DIM // LANES):
        lo = cc * LANES
        qi_ref[0, :, lo:lo + LANES] = _rope64(pm[:, V_END + lo:V_END + lo + LANES], c64, s64lo, s64hi)
    pt = _dot(h, wt_ref[...])
    kw_ref[0] = _rope64(pt, ct_ref[...], stlo_ref[...], sthi_ref[...])


def _rope_tables(t_len, past):
    pos = jnp.arange(past, past + t_len, dtype=jnp.int32).astype(F32)[:, None]

    def cs(d):
        half = d // 2
        inv = ROPE_THETA ** (-2.0 * jnp.arange(half, dtype=F32) / d)
        ang = pos * inv[None, :]
        return jnp.cos(ang), jnp.sin(ang)

    c, s = cs(HEAD_DIM)
    c128 = jnp.concatenate([c, c], axis=1)
    s128 = jnp.concatenate([-s, s], axis=1)
    c, s = cs(IDX_DIM)
    z = jnp.zeros_like(s)
    c64 = jnp.concatenate([c, c, c, c], axis=1)
    s64lo = jnp.concatenate([-s, z, -s, z], axis=1)
    s64hi = jnp.concatenate([z, s, z, s], axis=1)
    wscale = jnp.full((t_len, LANES - IDX_DIM), IDX_HEADS ** -0.5, F32)
    ct = jnp.concatenate([c, c, wscale], axis=1)
    stlo = jnp.concatenate([-s, z, z, z], axis=1)
    sthi = jnp.concatenate([z, s, z, z], axis=1)
    return c128, s128, c64, s64lo, s64hi, ct, stlo, sthi


def _attn_proj(x, mod, g, w_main, w_tail, tables, tm):
    b, t, d = x.shape
    tab_spec = pl.BlockSpec((tm, LANES), lambda bi, i: (i, 0))
    const2 = lambda bi, i: (0, 0)
    row3 = lambda bi, i: (bi, i, 0)
    return pl.pallas_call(
        _attn_proj_kernel,
        name="attn_proj",
        out_shape=(jax.ShapeDtypeStruct((b, t, Q_END), F32),
                   jax.ShapeDtypeStruct((b, t, K_END - Q_END), F32),
                   jax.ShapeDtypeStruct((b, t, V_END - K_END), F32),
                   jax.ShapeDtypeStruct((b, t, QI_END - V_END), F32),
                   jax.ShapeDtypeStruct((b, t, LANES), F32)),
        grid=(b, t // tm),
        in_specs=[pl.BlockSpec((1, tm, d), row3),
                  pl.BlockSpec((1, 6, d), lambda bi, i: (bi, 0, 0)),
                  pl.BlockSpec((1, d), const2),
                  pl.BlockSpec((d, QI_END), const2),
                  pl.BlockSpec((d, LANES), const2)] + [tab_spec] * 8,
        out_specs=(pl.BlockSpec((1, tm, Q_END), row3),
                   pl.BlockSpec((1, tm, K_END - Q_END), row3),
                   pl.BlockSpec((1, tm, V_END - K_END), row3),
                   pl.BlockSpec((1, tm, QI_END - V_END), row3),
                   pl.BlockSpec((1, tm, LANES), row3)),
        compiler_params=_cparams(("parallel", "parallel")),
    )(x, mod, g, w_main, w_tail, *tables)


def _attn_core_kernel(q_ref, qi_ref, kw_ref, k_ref, v_ref, ki_ref, x_ref, mod_ref, wo_ref, o_ref,
                      *, pos0, l_true, n_sel):
    tq = q_ref.shape[1]
    lp = k_ref.shape[1]
    i = pl.program_id(1)
    qpos = pos0 + i * tq + lax.broadcasted_iota(jnp.int32, (tq, 1), 0)
    kidx = lax.broadcasted_iota(jnp.int32, (tq, lp), 1)
    adm = (lax.shift_right_logical(kidx, 6) <= lax.shift_right_logical(qpos, 6)) & (kidx < l_true)
    sel = adm if l_true <= n_sel else _topk_mask(qi_ref, kw_ref, ki_ref, adm, kidx, n_sel)

    outs = []
    for kv in range(N_KV_HEADS):
        kk = k_ref[0, :, kv * HEAD_DIM:(kv + 1) * HEAD_DIM].astype(BF16)
        vv = v_ref[0, :, kv * HEAD_DIM:(kv + 1) * HEAD_DIM].astype(BF16)
        for gi in range(GROUP):
            hq = kv * GROUP + gi
            qh = q_ref[0, :, hq * HEAD_DIM:(hq + 1) * HEAD_DIM].astype(BF16)
            s = jnp.where(sel, _dot_nt(qh, kk), -jnp.inf)
            m = jnp.max(s, axis=1, keepdims=True)
            p = jnp.exp2(s - m)
            l = jnp.sum(p, axis=1, keepdims=True)
            outs.append(_dot(p.astype(BF16), vv) / l)
    o = jnp.concatenate(outs, axis=1).astype(BF16)
    o_ref[0] = x_ref[0] + mod_ref[0, 2:3, :] * _dot(o, wo_ref[...])


def _topk_mask(qi_ref, kw_ref, ki_ref, adm, kidx, n_sel):
    tq, lp = kidx.shape
    kib = ki_ref[0].astype(BF16)
    score = jnp.zeros((tq, lp), F32)
    for hh in range(IDX_HEADS):
        qh = qi_ref[0, :, hh * IDX_DIM:(hh + 1) * IDX_DIM].astype(BF16)
        lg = _dot_nt(qh, kib)
        wi = kw_ref[0, :, IDX_DIM + hh:IDX_DIM + hh + 1]
        score = score + wi * jnp.maximum(lg, 0.0)

    bits = lax.bitcast_convert_type(score + 0.0, jnp.int32)
    key = bits ^ ((bits >> 31) & jnp.int32(0x7FFFFFFF))
    key = jnp.where(adm, key, jnp.int32(INT_MIN))
    kf = jnp.float32(n_sel)

    def thr_body(it, t):
        cand = t + lax.shift_left(jnp.int32(1), 31 - it)
        cnt = jnp.sum(jnp.where(key >= cand, 1.0, 0.0), axis=1, keepdims=True)
        return jnp.where(cnt >= kf, cand, t)

    thr = lax.fori_loop(0, 32, thr_body, jnp.full((tq, 1), INT_MIN, jnp.int32))
    gt = key > thr
    eq = (key == thr) & adm
    need = kf - jnp.sum(jnp.where(gt, 1.0, 0.0), axis=1, keepdims=True)
    eqf = jnp.where(eq, 1.0, 0.0)
    excess = jnp.sum(eqf, axis=1, keepdims=True) - need

    def tie_search():
        nbits = lp.bit_length()

        def body(it, m):
            cand = m + lax.shift_left(jnp.int32(1), nbits - 1 - it)
            f = jnp.sum(jnp.where(kidx < cand, eqf, 0.0), axis=1, keepdims=True)
            return jnp.where(f < need, cand, m)

        return lax.fori_loop(0, nbits, body, jnp.zeros((tq, 1), jnp.int32)) + 1

    jstar = lax.cond(jnp.max(excess) > 0.0, tie_search, lambda: jnp.full((tq, 1), lp, jnp.int32))
    return gt | (eq & (kidx < jstar))


def _attn_core(q, qi, kw, k_all, v_all, ki_all, x, mod, w_out, *, past, row0, rows, n_sel, tq):
    b, t, d = x.shape
    l_true = past + row0 + rows
    lp = -(-l_true // LANES) * LANES
    blk0 = row0 // tq
    row3 = lambda bi, i: (bi, blk0 + i, 0)
    per_b = lambda bi, i: (bi, 0, 0)
    kern = functools.partial(_attn_core_kernel, pos0=past + row0, l_true=l_true, n_sel=n_sel)
    return pl.pallas_call(
        kern,
        name=f"attn_core_l{lp}",
        out_shape=jax.ShapeDtypeStruct((b, t, d), F32),
        input_output_aliases={6: 0},
        grid=(b, rows // tq),
        in_specs=[pl.BlockSpec((1, tq, Q_END), row3),
                  pl.BlockSpec((1, tq, QI_END - V_END), row3),
                  pl.BlockSpec((1, tq, LANES), row3),
                  pl.BlockSpec((1, lp, K_END - Q_END), per_b),
                  pl.BlockSpec((1, lp, V_END - K_END), per_b),
                  pl.BlockSpec((1, lp, IDX_DIM), per_b),
                  pl.BlockSpec((1, tq, d), row3),
                  pl.BlockSpec((1, 6, d), per_b),
                  pl.BlockSpec((N_HEADS * HEAD_DIM, d), lambda bi, i: (0, 0))],
        out_specs=pl.BlockSpec((1, tq, d), row3),
        compiler_params=_cparams(("parallel", "parallel")),
    )(q, qi, kw, k_all, v_all, ki_all, x, mod, w_out)


def _conv_mixer_kernel(x_ref, mod_ref, g_ref, win_ref, cw_ref, wout_ref, left_ref, o_ref, st_ref, carry_ref):
    d = D_MODEL
    tm = x_ref.shape[1]

    @pl.when(pl.program_id(1) == 0)
    def _():
        carry_ref[...] = jnp.zeros_like(carry_ref)
        carry_ref[6:8, :] = left_ref[0]

    x = x_ref[0]
    h = _norm_mod(x, g_ref[...], mod_ref[0, 1:2, :], mod_ref[0, 0:1, :]).astype(BF16)
    gb = _dot(h, win_ref[:, 0:d])
    gc = _dot(h, win_ref[:, d:2 * d])
    u = gc * _dot(h, win_ref[:, 2 * d:3 * d])
    y = _causal_conv3(u, carry_ref[...], cw_ref[0:1, :], cw_ref[1:2, :], cw_ref[2:3, :])
    carry_ref[...] = u[tm - SUBLANES:tm, :]
    st_ref[0] = u[tm - 2:tm, :]
    o_ref[0] = x + mod_ref[0, 2:3, :] * _dot((gb * y).astype(BF16), wout_ref[...])


def _conv_mixer(x, mod, g, w_in, conv_w, w_out, left, tm):
    b, t, d = x.shape
    row3 = lambda bi, i: (bi, i, 0)
    per_b = lambda bi, i: (bi, 0, 0)
    const2 = lambda bi, i: (0, 0)
    return pl.pallas_call(
        _conv_mixer_kernel,
        name="conv_mixer",
        out_shape=(jax.ShapeDtypeStruct((b, t, d), F32), jax.ShapeDtypeStruct((b, 2, d), F32)),
        grid=(b, t // tm),
        in_specs=[pl.BlockSpec((1, tm, d), row3),
                  pl.BlockSpec((1, 6, d), per_b),
                  pl.BlockSpec((1, d), const2),
                  pl.BlockSpec((d, 3 * d), const2),
                  pl.BlockSpec((3, d), const2),
                  pl.BlockSpec((d, d), const2),
                  pl.BlockSpec((1, 2, d), per_b)],
        out_specs=(pl.BlockSpec((1, tm, d), row3), pl.BlockSpec((1, 2, d), per_b)),
        scratch_shapes=[pltpu.VMEM((SUBLANES, d), F32)],
        compiler_params=_cparams(("parallel", "arbitrary")),
    )(x, mod, g, w_in, conv_w, w_out, left)


FFN_COLS = 512
FFN_ROWS = 512


def _ffn_kernel(x_ref, mod_ref, g_ref, wup_ref, cw_ref, cb_ref, wdn_ref, left_ref, gfin_ref, o_ref, st_ref,
                carry_ref, *, final_norm):
    tm = x_ref.shape[1]

    @pl.when(pl.program_id(1) == 0)
    def _():
        carry_ref[...] = jnp.zeros_like(carry_ref)
        carry_ref[6:8, :] = left_ref[0]

    x = x_ref[0]
    h = _norm_mod(x, g_ref[...], mod_ref[0, 4:5, :], mod_ref[0, 3:4, :]).astype(BF16)

    def conv_cols(lo):
        u = _dot(h, wup_ref[:, lo:lo + FFN_COLS])
        z = _causal_conv3(u, carry_ref[:, lo:lo + FFN_COLS], cw_ref[0:1, lo:lo + FFN_COLS],
                          cw_ref[1:2, lo:lo + FFN_COLS], cw_ref[2:3, lo:lo + FFN_COLS]) + cb_ref[:, lo:lo + FFN_COLS]
        carry_ref[:, lo:lo + FFN_COLS] = u[tm - SUBLANES:tm, :]
        st_ref[0, :, lo:lo + FFN_COLS] = u[tm - 2:tm, :]
        return z

    acc = jnp.zeros((tm, D_MODEL), F32)
    for c in range(0, D_FF, FFN_COLS):
        gate = conv_cols(c)
        val = conv_cols(D_FF + c)
        act = (gate * jax.nn.sigmoid(gate)) * val
        acc = acc + _dot(act.astype(BF16), wdn_ref[c:c + FFN_COLS, :])
    y = x + mod_ref[0, 5:6, :] * acc
    if final_norm:
        ms = jnp.mean(y * y, axis=-1, keepdims=True)
        y = y * lax.rsqrt(ms + NORM_EPS) * gfin_ref[...]
    o_ref[0] = y


def _ffn(x, mod, g, w_up, conv_w, conv_b, w_down, left, g_final, tm, final_norm):
    b, t, d = x.shape
    row3 = lambda bi, i: (bi, i, 0)
    per_b = lambda bi, i: (bi, 0, 0)
    const2 = lambda bi, i: (0, 0)
    kern = functools.partial(_ffn_kernel, final_norm=final_norm)
    return pl.pallas_call(
        kern,
        name="conv_ffn",
        out_shape=(jax.ShapeDtypeStruct((b, t, d), F32), jax.ShapeDtypeStruct((b, 2, 2 * D_FF), F32)),
        grid=(b, t // tm),
        in_specs=[pl.BlockSpec((1, tm, d), row3),
                  pl.BlockSpec((1, 6, d), per_b),
                  pl.BlockSpec((1, d), const2),
                  pl.BlockSpec((d, 2 * D_FF), const2),
                  pl.BlockSpec((3, 2 * D_FF), const2),
                  pl.BlockSpec((1, 2 * D_FF), const2),
                  pl.BlockSpec((D_FF, d), const2),
                  pl.BlockSpec((1, 2, 2 * D_FF), per_b),
                  pl.BlockSpec((1, d), const2)],
        out_specs=(pl.BlockSpec((1, tm, d), row3), pl.BlockSpec((1, 2, 2 * D_FF), per_b)),
        scratch_shapes=[pltpu.VMEM((SUBLANES, 2 * D_FF), F32)],
        compiler_params=_cparams(("parallel", "arbitrary")),
    )(x, mod, g, w_up, conv_w, conv_b, w_down, left, g_final)


def _softplus(z):
    return jnp.maximum(z, 0.0) + jnp.log1p(jnp.exp(-jnp.abs(z)))


def _rwkv_proj_kernel(x_ref, mod_ref, g_ref, sp_ref, mix_ref, wrkv_ref, w1_ref, w2_ref, w0_ref, a1_ref, a2_ref,
                      a0_ref, g1_ref, g2_ref, r_ref, k_ref, v_ref, w_ref, a_ref, gg_ref, sh_ref, carry_ref):
    tm = x_ref.shape[1]

    @pl.when(pl.program_id(1) == 0)
    def _():
        carry_ref[...] = jnp.zeros_like(carry_ref)
        carry_ref[7:8, :] = sp_ref[0]

    h = _norm_mod(x_ref[0], g_ref[...], mod_ref[0, 1:2, :], mod_ref[0, 0:1, :])
    (hs,) = _shifted(h, carry_ref[...], 1)
    carry_ref[...] = h[tm - SUBLANES:tm, :]
    sh_ref[0] = h[tm - 1:tm, :]
    xx = hs - h

    def mixed(j):
        return (h + xx * mix_ref[j:j + 1, :]).astype(BF16)

    r_ref[0] = _dot(mixed(0), wrkv_ref[0])
    k_ref[0] = _dot(mixed(2), wrkv_ref[1])
    v_ref[0] = _dot(mixed(3), wrkv_ref[2])
    lw = _dot(jnp.tanh(_dot(mixed(1), w1_ref[...])).astype(BF16), w2_ref[...])
    w_log = -_softplus(-(w0_ref[...] + lw)) - 0.5
    w_ref[0] = jnp.exp(-jnp.exp(w_log))
    a_ref[0] = jax.nn.sigmoid(a0_ref[...] + _dot(_dot(mixed(4), a1_ref[...]).astype(BF16), a2_ref[...]))
    gg_ref[0] = _dot(jax.nn.sigmoid(_dot(mixed(5), g1_ref[...])).astype(BF16), g2_ref[...])


def _rwkv_proj(x, mod, g, shift_prev, mix, w_rkv, w1, w2, w0, a1, a2, a0, g1, g2, tm):
    b, t, d = x.shape
    row3 = lambda bi, i: (bi, i, 0)
    per_b = lambda bi, i: (bi, 0, 0)
    const2 = lambda bi, i: (0, 0)
    big = jax.ShapeDtypeStruct((b, t, d), F32)
    lora = w1.shape[1]
    gl = g1.shape[1]
    return pl.pallas_call(
        _rwkv_proj_kernel,
        name="rwkv_proj",
        out_shape=(big,) * 6 + (jax.ShapeDtypeStruct((b, 1, d), F32),),
        grid=(b, t // tm),
        in_specs=[pl.BlockSpec((1, tm, d), row3),
                  pl.BlockSpec((1, 6, d), per_b),
                  pl.BlockSpec((1, d), const2),
                  pl.BlockSpec((1, 1, d), per_b),
                  pl.BlockSpec((6, d), const2),
                  pl.BlockSpec((3, d, d), lambda bi, i: (0, 0, 0)),
                  pl.BlockSpec((d, lora), const2),
                  pl.BlockSpec((lora, d), const2),
                  pl.BlockSpec((1, d), const2),
                  pl.BlockSpec((d, lora), const2),
                  pl.BlockSpec((lora, d), const2),
                  pl.BlockSpec((1, d), const2),
                  pl.BlockSpec((d, gl), const2),
                  pl.BlockSpec((gl, d), const2)],
        out_specs=(pl.BlockSpec((1, tm, d), row3),) * 6 + (pl.BlockSpec((1, 1, d), per_b),),
        scratch_shapes=[pltpu.VMEM((SUBLANES, d), F32)],
        compiler_params=_cparams(("parallel", "arbitrary")),
    )(x, mod, g, shift_prev, mix, w_rkv, w1, w2, w0, a1, a2, a0, g1, g2)


SCAN_STEPS = 32
def _rwkv_scan_kernel(r_ref, k_ref, v_ref, w_ref, a_ref, kk_ref, ka_ref, rk_ref, lg_ref, lb_ref, s0_ref,
                      y_ref, sfin_ref, s_ref, bc_ref):
    tc = r_ref.shape[0]
    n = RW_HEAD

    @pl.when(pl.program_id(1) == 0)
    def _():
        s_ref[...] = s0_ref[...]

    def unit_kk(t):
        kkr = k_ref[t] * kk_ref[...]
        nrm = jnp.sqrt(jnp.sum(kkr * kkr, axis=0, keepdims=True))
        return kkr / jnp.maximum(nrm, 1e-12)

    def step(t, carry, has_next):
        cum_prev, sa, kk = carry
        kt = k_ref[t]
        at = a_ref[t]
        rt = r_ref[t]
        vt = v_ref[t]
        cum = cum_prev * w_ref[t]
        inv = 1.0 / cum
        kmod = kt * (1.0 + (at - 1.0) * ka_ref[...])
        bc_ref[0] = (kk * at) * inv
        bc_ref[1] = kmod * inv
        bc_ref[2] = rt * cum
        if has_next:
            kk_next = unit_kk(t + 1)
            bc_ref[3] = -kk_next * cum
        y = jnp.zeros((n, LANES), F32)
        sa_next = jnp.zeros((n, LANES), F32)
        for j in range(n):
            sj = s_ref[j] + sa * bc_ref[0, j:j + 1, :] + vt * bc_ref[1, j:j + 1, :]
            s_ref[j] = sj
            y = y + sj * bc_ref[2, j:j + 1, :]
            if has_next:
                sa_next = sa_next + sj * bc_ref[3, j:j + 1, :]
        mu = jnp.mean(y, axis=0, keepdims=True)
        yc = y - mu
        var = jnp.mean(yc * yc, axis=0, keepdims=True)
        yn = (yc * lax.rsqrt(var + RW_GN_EPS)) * lg_ref[...] + lb_ref[...]
        bonus = jnp.sum(rt * kmod * rk_ref[...], axis=0, keepdims=True) * vt
        y_ref[t] = yn + bonus
        if has_next:
            return cum, sa_next, kk_next
        return cum

    kk0 = unit_kk(0)
    bc_ref[3] = -kk0
    sa0 = jnp.zeros((n, LANES), F32)
    for j in range(n):
        sa0 = sa0 + s_ref[j] * bc_ref[3, j:j + 1, :]
    carry = (jnp.ones((n, LANES), F32), sa0, kk0)
    carry = lax.fori_loop(0, tc - 1, lambda t, c: step(t, c, True), carry)
    bc_ref[0] = step(tc - 1, carry, False)
    for j in range(n):
        s_ref[j] = s_ref[j] * bc_ref[0, j:j + 1, :]
    sfin_ref[...] = s_ref[...]


def _rwkv_scan(r, k, v, w, a, kk_t, ka_t, rk_t, lg_t, lb_t, s0, tc):
    t, n, bh = r.shape
    seq = pl.BlockSpec((tc, n, LANES), lambda lb, ti: (ti, 0, lb))
    par = pl.BlockSpec((n, LANES), lambda lb, ti: (0, lb))
    st = pl.BlockSpec((n, n, LANES), lambda lb, ti: (0, 0, lb))
    return pl.pallas_call(
        _rwkv_scan_kernel,
        name="rwkv_scan",
        out_shape=(jax.ShapeDtypeStruct((t, n, bh), F32), jax.ShapeDtypeStruct((n, n, bh), F32)),
        grid=(bh // LANES, t // tc),
        in_specs=[seq] * 5 + [par] * 5 + [st],
        out_specs=(seq, st),
        scratch_shapes=[pltpu.VMEM((n, n, LANES), F32), pltpu.VMEM((5, n, LANES), F32)],
        compiler_params=_cparams(("parallel", "arbitrary")),
    )(r, k, v, w, a, kk_t, ka_t, rk_t, lg_t, lb_t, s0)


def _rwkv_out_kernel(y_ref, gg_ref, x_ref, mod_ref, wo_ref, o_ref):
    o_ref[0] = x_ref[0] + mod_ref[0, 2:3, :] * _dot((y_ref[0] * gg_ref[0]).astype(BF16), wo_ref[...])


def _rwkv_out(y, gg, x, mod, w_o, tm):
    b, t, d = x.shape
    row3 = lambda bi, i: (bi, i, 0)
    return pl.pallas_call(
        _rwkv_out_kernel,
        name="rwkv_out",
        out_shape=jax.ShapeDtypeStruct((b, t, d), F32),
        grid=(b, t // tm),
        in_specs=[pl.BlockSpec((1, tm, d), row3), pl.BlockSpec((1, tm, d), row3), pl.BlockSpec((1, tm, d), row3),
                  pl.BlockSpec((1, 6, d), lambda bi, i: (bi, 0, 0)),
                  pl.BlockSpec((d, d), lambda bi, i: (0, 0))],
        out_specs=pl.BlockSpec((1, tm, d), row3),
        compiler_params=_cparams(("parallel", "parallel")),
    )(y, gg, x, mod, w_o)


def _pad_rows(a, rows):
    return jnp.pad(a, ((0, 0), (0, rows - a.shape[1]), (0, 0)))


def _trunk(x, mods, states, ffn_left, wts, past):
    b, t, d = x.shape
    tm = min(t, 256)
    tq = min(t, 256)
    g_norm = wts['g_norm']
    new_states = []
    new_ffn = []
    tables = _rope_tables(t, past)
    for i in range(DEPTH):
        mod = mods[i]
        kind = i % 3
        g_a = g_norm[i, 0][None, :]
        if kind == 0:
            j = i // 3
            past_k, past_v, past_ki = states[i]
            q, k, v, qi, kw = _attn_proj(x, mod, g_a, wts['attn_w_main'][j], wts['attn_w_tail'][j], tables, tm)
            ki = kw[:, :, :IDX_DIM]
            l_all = past + t
            if past:
                lp = -(-l_all // LANES) * LANES
                kvd = N_KV_HEADS * HEAD_DIM
                k_all = _pad_rows(jnp.concatenate([past_k.reshape(b, past, kvd), k], axis=1), lp)
                v_all = _pad_rows(jnp.concatenate([past_v.reshape(b, past, kvd), v], axis=1), lp)
                ki_all = _pad_rows(jnp.concatenate([past_ki, ki], axis=1), lp)
            else:
                k_all, v_all, ki_all = k, v, ki
            for row0 in range(0, t, tq):
                x = _attn_core(q, qi, kw, k_all, v_all, ki_all, x, mod, wts['attn_w_out'][j], past=past,
                               row0=row0, rows=tq, n_sel=min(TOPK_MAX, l_all // 4), tq=tq)
            new_states.append((k.reshape(b, t, N_KV_HEADS, HEAD_DIM), v.reshape(b, t, N_KV_HEADS, HEAD_DIM), ki))
        elif kind == 1:
            x, conv_st = _conv_mixer(x, mod, g_a, wts['sc_w_in'], wts['sc_conv_w'], wts['sc_w_out'],
                                     states[i][0], tm)
            new_states.append((conv_st,))
        else:
            shift_prev, wkv0 = states[i]
            r, k, v, w, a, gg, shift_new = _rwkv_proj(
                x, mod, g_a, shift_prev, wts['rw_mix'], wts['rw_w_rkv'], wts['rw_w1'], wts['rw_w2'], wts['rw_w0'],
                wts['rw_a1'], wts['rw_a2'], wts['rw_a0'], wts['rw_g1'], wts['rw_g2'], tm)
            bh = b * RW_HEADS

            def to_scan(z):
                return z.reshape(b, t, RW_HEADS, RW_HEAD).transpose(1, 3, 0, 2).reshape(t, RW_HEAD, bh)

            def head_tile(p):
                return jnp.tile(p.reshape(RW_HEADS, RW_HEAD).T, (1, b))

            s0 = wkv0.transpose(3, 2, 0, 1).reshape(RW_HEAD, RW_HEAD, bh)
            y, s_fin = _rwkv_scan(to_scan(r), to_scan(k), to_scan(v), to_scan(w), to_scan(a),
                                  head_tile(wts['rw_k_k']), head_tile(wts['rw_k_a']), head_tile(wts['rw_r_k']),
                                  head_tile(wts['rw_ln_g']), head_tile(wts['rw_ln_b']), s0, min(t, SCAN_STEPS))
            y = y.reshape(t, RW_HEAD, b, RW_HEADS).transpose(2, 0, 3, 1).reshape(b, t, d)
            wkv_new = s_fin.reshape(RW_HEAD, RW_HEAD, b, RW_HEADS).transpose(2, 3, 1, 0)
            x = _rwkv_out(y, gg, x, mod, wts['rw_w_o'], tm)
            new_states.append((shift_new, wkv_new))
        x, f_st = _ffn(x, mod, g_norm[i, 1][None, :], wts['ffn_w_up'][i], wts['ffn_conv_w'][i],
                       wts['ffn_conv_b'][i][None, :], wts['ffn_w_down'][i], ffn_left[i],
                       wts['g_final'][None, :], min(t, FFN_ROWS), i == DEPTH - 1)
        new_ffn.append(f_st)
    return x, new_states, jnp.stack(new_ffn)


def _pad_cols(w, cols):
    return jnp.pad(w, ((0, 0), (0, cols - w.shape[1])))


def _pad_rows2(w, rows):
    return jnp.pad(w, ((0, rows - w.shape[0]), (0, 0)))


def kernel(x_prompt, x_sample, c_prompt, c_sample, cache_k_0, cache_v_0, cache_kidx_0, state_conv_1, state_shift_2, state_wkv_2, cache_k_3, cache_v_3, cache_kidx_3, state_ffn_conv, w_mod, b_mod, g_norm, g_final, attn_w_in, attn_w_out, sc_w_in, sc_conv_w, sc_w_out, rw_mix, rw_w_rkv, rw_w_o, rw_w0, rw_w1, rw_w2, rw_a0, rw_a1, rw_a2, rw_g1, rw_g2, rw_k_k, rw_k_a, rw_r_k, rw_ln_g, rw_ln_b, ffn_w_up, ffn_conv_w, ffn_conv_b, ffn_w_down):
    d = D_MODEL
    bp = x_prompt.shape[0]
    bs = x_sample.shape[0]
    dt = x_prompt.dtype

    wts = _prep_weights(g_norm, g_final, attn_w_in, attn_w_out, sc_w_in, sc_conv_w, sc_w_out, rw_mix, rw_w_rkv,
                        rw_w_o, rw_w0, rw_w1, rw_w2, rw_a0, rw_a1, rw_a2, rw_g1, rw_g2, rw_k_k, rw_k_a, rw_r_k,
                        rw_ln_g, rw_ln_b, ffn_w_up, ffn_conv_w, ffn_conv_b, ffn_w_down)

    mods = _mod_all(jnp.concatenate([c_prompt, c_sample], axis=0), w_mod, b_mod)
    mods_p = mods[:, :bp].reshape(DEPTH, bp, 6, d)
    mods_s = mods[:, bp:].reshape(DEPTH, bs, 6, d)

    ffn0 = jnp.zeros((DEPTH, bp, 2, 2 * D_FF), dt)
    y_p, st_p, ffn_p = _trunk(x_prompt, mods_p, _empty_states(bp, dt), ffn0, wts, 0)
    sample_states = [(cache_k_0, cache_v_0, cache_kidx_0), (state_conv_1,), (state_shift_2, state_wkv_2),
                     (cache_k_3, cache_v_3, cache_kidx_3)]
    y_s, st_s, ffn_s = _trunk(x_sample, mods_s, sample_states, state_ffn_conv, wts, cache_k_0.shape[1])
    (k0_p, v0_p, ki0_p), (conv1_p,), (shift2_p, wkv2_p), (k3_p, v3_p, ki3_p) = st_p
    (k0_s, v0_s, ki0_s), (conv1_s,), (shift2_s, wkv2_s), (k3_s, v3_s, ki3_s) = st_s
    return (y_p, y_s,
            k0_p, v0_p, ki0_p, conv1_p, shift2_p, wkv2_p, k3_p, v3_p, ki3_p, ffn_p,
            k0_s, v0_s, ki0_s, conv1_s, shift2_s, wkv2_s, k3_s, v3_s, ki3_s, ffn_s)


def _empty_states(b, dt):
    d = D_MODEL
    kv = (jnp.zeros((b, 0, N_KV_HEADS, HEAD_DIM), dt), jnp.zeros((b, 0, N_KV_HEADS, HEAD_DIM), dt),
          jnp.zeros((b, 0, IDX_DIM), dt))
    return [kv, (jnp.zeros((b, 2, d), dt),),
            (jnp.zeros((b, 1, d), dt), jnp.zeros((b, RW_HEADS, RW_HEAD, RW_HEAD), dt)), kv]


def _prep_weights(g_norm, g_final, attn_w_in, attn_w_out, sc_w_in, sc_conv_w, sc_w_out, rw_mix, rw_w_rkv, rw_w_o,
                  rw_w0, rw_w1, rw_w2, rw_a0, rw_a1, rw_a2, rw_g1, rw_g2, rw_k_k, rw_k_a, rw_r_k, rw_ln_g, rw_ln_b,
                  ffn_w_up, ffn_conv_w, ffn_conv_b, ffn_w_down):
    return {
        'g_norm': g_norm, 'g_final': g_final,
        'attn_w_main': attn_w_in[:, :, :QI_END].astype(BF16),
        'attn_w_tail': jnp.pad(attn_w_in[:, :, QI_END:], ((0, 0), (0, 0), (0, LANES - (WI_END - QI_END)))).astype(BF16),
        'attn_w_out': attn_w_out.astype(BF16),
        'sc_w_in': sc_w_in.astype(BF16), 'sc_conv_w': sc_conv_w, 'sc_w_out': sc_w_out.astype(BF16),
        'rw_mix': rw_mix, 'rw_w_rkv': rw_w_rkv.astype(BF16), 'rw_w_o': rw_w_o.astype(BF16),
        'rw_w0': rw_w0[None, :], 'rw_w1': _pad_cols(rw_w1, LANES).astype(BF16),
        'rw_w2': _pad_rows2(rw_w2, LANES).astype(BF16),
        'rw_a0': rw_a0[None, :], 'rw_a1': _pad_cols(rw_a1, LANES).astype(BF16),
        'rw_a2': _pad_rows2(rw_a2, LANES).astype(BF16),
        'rw_g1': rw_g1.astype(BF16), 'rw_g2': rw_g2.astype(BF16),
        'rw_k_k': rw_k_k, 'rw_k_a': rw_k_a, 'rw_r_k': rw_r_k, 'rw_ln_g': rw_ln_g, 'rw_ln_b': rw_ln_b,
        'ffn_w_up': ffn_w_up.astype(BF16), 'ffn_conv_w': ffn_conv_w, 'ffn_conv_b': ffn_conv_b,
        'ffn_w_down': ffn_w_down.astype(BF16),
    }
```

```python
import functools

import jax
import jax.numpy as jnp
from jax import lax
from jax.experimental import pallas as pl
from jax.experimental.pallas import tpu as pltpu

D_MODEL = 1024
DEPTH = 4
CHUNK = 64
N_HEADS = 8
HEAD_DIM = 128
N_KV_HEADS = 2
GROUP = N_HEADS // N_KV_HEADS
IDX_HEADS = 8
IDX_DIM = 64
TOPK_MAX = 256
ROPE_THETA = 10000.0
RW_HEAD = 64
RW_HEADS = D_MODEL // RW_HEAD
RW_GN_EPS = 64e-5
D_FF = 2 * D_MODEL
NORM_EPS = 1e-6

LANES = 128
SUBLANES = 8
VMEM_LIMIT = 52 * 1024 * 1024
INT_MIN = -2 ** 31
F32_TINY = 1.1754943508222875e-38

F32 = jnp.float32
BF16 = jnp.bfloat16

Q_END = N_HEADS * HEAD_DIM
K_END = Q_END + N_KV_HEADS * HEAD_DIM
V_END = K_END + N_KV_HEADS * HEAD_DIM
QI_END = V_END + IDX_HEADS * IDX_DIM
KI_END = QI_END + IDX_DIM
WI_END = KI_END + IDX_HEADS
Q_SCALE = HEAD_DIM ** -0.5 * 1.4426950408889634


def _cparams(sem):
    return pltpu.CompilerParams(dimension_semantics=sem, vmem_limit_bytes=VMEM_LIMIT)


def _wspec(shape):
    return pl.BlockSpec(shape, lambda bi, i: (0,) * len(shape), pipeline_mode=pl.Buffered(1))


def _dot(a, b):
    return jnp.dot(a, b, preferred_element_type=F32)


def _dot_nt(a, b):
    return lax.dot_general(a, b, (((1,), (1,)), ((), ())), preferred_element_type=F32)


def _norm_mod(x, g, scale, shift):
    ms = jnp.mean(x * x, axis=-1, keepdims=True)
    return (x * lax.rsqrt(ms + NORM_EPS) * g) * (1.0 + scale) + shift


def _shifted(u, hist, n):
    tm, c = u.shape
    tiles = jnp.concatenate([hist, u], axis=0).reshape(tm // SUBLANES + 1, SUBLANES, c)
    row = lax.broadcasted_iota(jnp.int32, (1, SUBLANES, 1), 1)
    outs = []
    for s in range(n, 0, -1):
        rot = pltpu.roll(tiles, s, 1)
        outs.append(jnp.where(row < s, rot[:-1], rot[1:]).reshape(tm, c))
    return outs


def _causal_conv3(u, hist, w0, w1, w2):
    s2, s1 = _shifted(u, hist, 2)
    return w0 * s2 + w1 * s1 + w2 * u


def _mod_kernel(c_ref, w_ref, b_ref, o_ref):
    o_ref[0] = _dot(c_ref[...].astype(BF16), w_ref[0].astype(BF16)) + b_ref[0]


def _mod_all(c_all, w_mod, b_mod):
    nb = c_all.shape[0]
    d = D_MODEL
    return pl.pallas_call(
        _mod_kernel,
        name="adaln_mod",
        out_shape=jax.ShapeDtypeStruct((DEPTH, nb, 6 * d), F32),
        grid=(DEPTH, 6),
        in_specs=[pl.BlockSpec((nb, d), lambda l, n: (0, 0)),
                  pl.BlockSpec((1, d, d), lambda l, n: (l, 0, n)),
                  pl.BlockSpec((1, 1, d), lambda l, n: (l, 0, n))],
        out_specs=pl.BlockSpec((1, nb, d), lambda l, n: (l, 0, n)),
        compiler_params=_cparams(("parallel", "parallel")),
    )(c_all, w_mod, b_mod.reshape(DEPTH, 1, 6 * d))


def _rope128(seg, cos, sin_signed):
    return seg * cos + pltpu.roll(seg, HEAD_DIM // 2, 1) * sin_signed


def _rope64(seg, cos, sin_lo, sin_hi):
    return seg * cos + pltpu.roll(seg, LANES - IDX_DIM // 2, 1) * sin_lo + pltpu.roll(seg, IDX_DIM // 2, 1) * sin_hi


def _attn_proj_kernel(x_ref, mod_ref, g_ref, wm_ref, wt_ref, c128_ref, s128_ref, c64_ref, s64lo_ref,
                      s64hi_ref, ct_ref, stlo_ref, sthi_ref, q_ref, k_ref, v_ref, qi_ref, kw_ref, k4_ref, v4_ref):
    h = _norm_mod(x_ref[0], g_ref[...], mod_ref[0, 1:2, :], mod_ref[0, 0:1, :]).astype(BF16)
    pm = _dot(h, wm_ref[...])
    c128 = c128_ref[...]
    s128 = s128_ref[...]
    for hh in range(N_HEADS):
        lo = hh * HEAD_DIM
        q_ref[0, :, lo:lo + HEAD_DIM] = _rope128(pm[:, lo:lo + HEAD_DIM], c128, s128) * Q_SCALE
    for hh in range(N_KV_HEADS):
        lo = hh * HEAD_DIM
        kh = _rope128(pm[:, Q_END + lo:Q_END + lo + HEAD_DIM], c128, s128)
        k_ref[0, :, lo:lo + HEAD_DIM] = kh
        k4_ref[0, :, hh, :] = kh
        v4_ref[0, :, hh, :] = pm[:, K_END + lo:K_END + lo + HEAD_DIM]
    v_ref[0] = pm[:, K_END:V_END]
    c64 = c64_ref[...]
    s64lo = s64lo_ref[...]
    s64hi = s64hi_ref[...]
    for cc in range(IDX_HEADS * IDX_DIM // LANES):
        lo = cc * LANES
        qi_ref[0, :, lo:lo + LANES] = _rope64(pm[:, V_END + lo:V_END + lo + LANES], c64, s64lo, s64hi)
    pt = _dot(h, wt_ref[...])
    kw_ref[0] = _rope64(pt, ct_ref[...], stlo_ref[...], sthi_ref[...])


def _rope_tables(t_len, past):
    pos = jnp.arange(past, past + t_len, dtype=jnp.int32).astype(F32)[:, None]

    def cs(d):
        half = d // 2
        inv = ROPE_THETA ** (-2.0 * jnp.arange(half, dtype=F32) / d)
        ang = pos * inv[None, :]
        return jnp.cos(ang), jnp.sin(ang)

    c, s = cs(HEAD_DIM)
    c128 = jnp.concatenate([c, c], axis=1)
    s128 = jnp.concatenate([-s, s], axis=1)
    c, s = cs(IDX_DIM)
    z = jnp.zeros_like(s)
    c64 = jnp.concatenate([c, c, c, c], axis=1)
    s64lo = jnp.concatenate([-s, z, -s, z], axis=1)
    s64hi = jnp.concatenate([z, s, z, s], axis=1)
    wscale = jnp.full((t_len, LANES - IDX_DIM), IDX_HEADS ** -0.5, F32)
    ct = jnp.concatenate([c, c, wscale], axis=1)
    stlo = jnp.concatenate([-s, z, z, z], axis=1)
    sthi = jnp.concatenate([z, s, z, z], axis=1)
    return c128, s128, c64, s64lo, s64hi, ct, stlo, sthi


def _attn_proj(x, mod, g, w_main, w_tail, tables, tm):
    b, t, d = x.shape
    tab_spec = pl.BlockSpec((tm, LANES), lambda bi, i: (i, 0))
    const2 = lambda bi, i: (0, 0)
    row3 = lambda bi, i: (bi, i, 0)
    return pl.pallas_call(
        _attn_proj_kernel,
        name="attn_proj",
        out_shape=(jax.ShapeDtypeStruct((b, t, Q_END), F32),
                   jax.ShapeDtypeStruct((b, t, K_END - Q_END), F32),
                   jax.ShapeDtypeStruct((b, t, V_END - K_END), F32),
                   jax.ShapeDtypeStruct((b, t, QI_END - V_END), F32),
                   jax.ShapeDtypeStruct((b, t, LANES), F32),
                   jax.ShapeDtypeStruct((b, t, N_KV_HEADS, HEAD_DIM), F32),
                   jax.ShapeDtypeStruct((b, t, N_KV_HEADS, HEAD_DIM), F32)),
        grid=(b, t // tm),
        in_specs=[pl.BlockSpec((1, tm, d), row3),
                  pl.BlockSpec((1, 6, d), lambda bi, i: (bi, 0, 0)),
                  pl.BlockSpec((1, d), const2),
                  _wspec((d, QI_END)),
                  _wspec((d, LANES))] + [tab_spec] * 8,
        out_specs=(pl.BlockSpec((1, tm, Q_END), row3),
                   pl.BlockSpec((1, tm, K_END - Q_END), row3),
                   pl.BlockSpec((1, tm, V_END - K_END), row3),
                   pl.BlockSpec((1, tm, QI_END - V_END), row3),
                   pl.BlockSpec((1, tm, LANES), row3),
                   pl.BlockSpec((1, tm, N_KV_HEADS, HEAD_DIM), lambda bi, i: (bi, i, 0, 0)),
                   pl.BlockSpec((1, tm, N_KV_HEADS, HEAD_DIM), lambda bi, i: (bi, i, 0, 0))),
        compiler_params=_cparams(("parallel", "parallel")),
    )(x, mod, g, w_main, w_tail, *tables)


def _attn_core_kernel(q_ref, qi_ref, kw_ref, k_ref, v_ref, ki_ref, x_ref, mod_ref, wo_ref, o_ref,
                      *, pos0, l_true, n_sel):
    tq = q_ref.shape[1]
    lp = k_ref.shape[1]
    i = pl.program_id(1)
    qpos = pos0 + i * tq + lax.broadcasted_iota(jnp.int32, (tq, 1), 0)
    kidx = lax.broadcasted_iota(jnp.int32, (tq, lp), 1)
    adm = (lax.shift_right_logical(kidx, 6) <= lax.shift_right_logical(qpos, 6)) & (kidx < l_true)
    sel = adm if l_true <= n_sel else _topk_mask(qi_ref, kw_ref, ki_ref, adm, kidx, n_sel)

    outs = []
    for kv in range(N_KV_HEADS):
        kk = k_ref[0, :, kv * HEAD_DIM:(kv + 1) * HEAD_DIM].astype(BF16)
        vv = v_ref[0, :, kv * HEAD_DIM:(kv + 1) * HEAD_DIM].astype(BF16)
        for gi in range(GROUP):
            hq = kv * GROUP + gi
            qh = q_ref[0, :, hq * HEAD_DIM:(hq + 1) * HEAD_DIM].astype(BF16)
            s = jnp.where(sel, _dot_nt(qh, kk), -jnp.inf)
            m = jnp.max(s, axis=1, keepdims=True)
            p = jnp.exp2(s - m)
            l = jnp.sum(p, axis=1, keepdims=True)
            outs.append(_dot(p.astype(BF16), vv) / l)
    o = jnp.concatenate(outs, axis=1).astype(BF16)
    o_ref[0] = x_ref[0] + mod_ref[0, 2:3, :] * _dot(o, wo_ref[...])


def _topk_mask(qi_ref, kw_ref, ki_ref, adm, kidx, n_sel):
    tq, lp = kidx.shape
    kib = ki_ref[0].astype(BF16)
    score = jnp.zeros((tq, lp), F32)
    for hh in range(IDX_HEADS):
        qh = qi_ref[0, :, hh * IDX_DIM:(hh + 1) * IDX_DIM].astype(BF16)
        lg = _dot_nt(qh, kib)
        wi = kw_ref[0, :, IDX_DIM + hh:IDX_DIM + hh + 1]
        score = score + wi * jnp.maximum(lg, 0.0)

    score = jnp.where(jnp.abs(score) < F32_TINY, 0.0, score)
    bits = lax.bitcast_convert_type(score, jnp.int32)
    key = bits ^ ((bits >> 31) & jnp.int32(0x7FFFFFFF))
    key = jnp.where(adm, key, jnp.int32(INT_MIN))
    kf = jnp.float32(n_sel)

    hi = jnp.where(adm, lax.bitcast_convert_type(bits & jnp.int32(-65536), F32), -jnp.inf).astype(BF16)
    hi_cols = [hi[:, c * LANES:(c + 1) * LANES] for c in range(lp // LANES)]
    one16 = jnp.ones((tq, LANES), BF16)
    zero16 = jnp.zeros((tq, LANES), BF16)

    def hi_body(it, t):
        cand = t + lax.shift_left(jnp.int32(1), 15 - it)
        ceff = jnp.where((cand > 0) & (cand < 128), 128, jnp.where((cand < 0) & (cand >= -128), 0, cand))
        cbits = lax.shift_left(ceff ^ ((ceff >> 15) & jnp.int32(0x7FFF)), 16)
        cval = jnp.broadcast_to(lax.bitcast_convert_type(cbits, F32), (tq, LANES)).astype(BF16)
        part = zero16
        for col in hi_cols:
            part = part + jnp.where(col >= cval, one16, zero16)
        cnt = jnp.sum(part.astype(F32), axis=1, keepdims=True)
        return jnp.where(cnt >= kf, cand, t)

    thr_hi = lax.fori_loop(0, 16, hi_body, jnp.full((tq, 1), -2 ** 15, jnp.int32))

    def lo_body(it, t):
        cand = t + lax.shift_left(jnp.int32(1), 15 - it)
        cnt = jnp.sum(jnp.where(key >= cand, 1.0, 0.0), axis=1, keepdims=True)
        return jnp.where(cnt >= kf, cand, t)

    thr = lax.fori_loop(0, 16, lo_body, lax.shift_left(thr_hi, 16))
    gt = key > thr
    eq = (key == thr) & adm
    need = kf - jnp.sum(jnp.where(gt, 1.0, 0.0), axis=1, keepdims=True)
    eqf = jnp.where(eq, 1.0, 0.0)
    excess = jnp.sum(eqf, axis=1, keepdims=True) - need

    def tie_search():
        nbits = lp.bit_length()

        def body(it, m):
            cand = m + lax.shift_left(jnp.int32(1), nbits - 1 - it)
            f = jnp.sum(jnp.where(kidx < cand, eqf, 0.0), axis=1, keepdims=True)
            return jnp.where(f < need, cand, m)

        return lax.fori_loop(0, nbits, body, jnp.zeros((tq, 1), jnp.int32)) + 1

    jstar = lax.cond(jnp.max(excess) > 0.0, tie_search, lambda: jnp.full((tq, 1), lp, jnp.int32))
    return gt | (eq & (kidx < jstar))


def _attn_core(q, qi, kw, k_all, v_all, ki_all, x, mod, w_out, *, past, row0, rows, n_sel, tq):
    b, t, d = x.shape
    l_true = past + row0 + rows
    lp = -(-l_true // LANES) * LANES
    blk0 = row0 // tq
    row3 = lambda bi, i: (bi, blk0 + i, 0)
    per_b = lambda bi, i: (bi, 0, 0)
    kern = functools.partial(_attn_core_kernel, pos0=past + row0, l_true=l_true, n_sel=n_sel)
    return pl.pallas_call(
        kern,
        name=f"attn_core_l{lp}",
        out_shape=jax.ShapeDtypeStruct((b, t, d), F32),
        input_output_aliases={6: 0},
        grid=(b, rows // tq),
        in_specs=[pl.BlockSpec((1, tq, Q_END), row3),
                  pl.BlockSpec((1, tq, QI_END - V_END), row3),
                  pl.BlockSpec((1, tq, LANES), row3),
                  pl.BlockSpec((1, lp, K_END - Q_END), per_b),
                  pl.BlockSpec((1, lp, V_END - K_END), per_b),
                  pl.BlockSpec((1, lp, IDX_DIM), per_b),
                  pl.BlockSpec((1, tq, d), row3),
                  pl.BlockSpec((1, 6, d), per_b),
                  _wspec((N_HEADS * HEAD_DIM, d))],
        out_specs=pl.BlockSpec((1, tq, d), row3),
        compiler_params=_cparams(("parallel", "parallel")),
    )(q, qi, kw, k_all, v_all, ki_all, x, mod, w_out)


def _conv_mixer_kernel(x_ref, mod_ref, g_ref, win_ref, cw_ref, wout_ref, left_ref, o_ref, st_ref, carry_ref):
    d = D_MODEL
    tm = x_ref.shape[1]

    @pl.when(pl.program_id(1) == 0)
    def _():
        carry_ref[...] = jnp.zeros_like(carry_ref)
        carry_ref[6:8, :] = left_ref[0]

    x = x_ref[0]
    h = _norm_mod(x, g_ref[...], mod_ref[0, 1:2, :], mod_ref[0, 0:1, :]).astype(BF16)
    gb = _dot(h, win_ref[:, 0:d])
    gc = _dot(h, win_ref[:, d:2 * d])
    u = gc * _dot(h, win_ref[:, 2 * d:3 * d])
    y = _causal_conv3(u, carry_ref[...], cw_ref[0:1, :], cw_ref[1:2, :], cw_ref[2:3, :])
    carry_ref[...] = u[tm - SUBLANES:tm, :]
    st_ref[0] = u[tm - 2:tm, :]
    o_ref[0] = x + mod_ref[0, 2:3, :] * _dot((gb * y).astype(BF16), wout_ref[...])


def _conv_mixer(x, mod, g, w_in, conv_w, w_out, left, tm):
    b, t, d = x.shape
    row3 = lambda bi, i: (bi, i, 0)
    per_b = lambda bi, i: (bi, 0, 0)
    const2 = lambda bi, i: (0, 0)
    return pl.pallas_call(
        _conv_mixer_kernel,
        name="conv_mixer",
        out_shape=(jax.ShapeDtypeStruct((b, t, d), F32), jax.ShapeDtypeStruct((b, 2, d), F32)),
        grid=(b, t // tm),
        in_specs=[pl.BlockSpec((1, tm, d), row3),
                  pl.BlockSpec((1, 6, d), per_b),
                  pl.BlockSpec((1, d), const2),
                  _wspec((d, 3 * d)),
                  pl.BlockSpec((3, d), const2),
                  _wspec((d, d)),
                  pl.BlockSpec((1, 2, d), per_b)],
        out_specs=(pl.BlockSpec((1, tm, d), row3), pl.BlockSpec((1, 2, d), per_b)),
        scratch_shapes=[pltpu.VMEM((SUBLANES, d), F32)],
        compiler_params=_cparams(("parallel", "arbitrary")),
    )(x, mod, g, w_in, conv_w, w_out, left)


FFN_COLS = 512
FFN_ROWS = 1024


def _ffn_kernel(x_ref, mod_ref, g_ref, wup_ref, cw_ref, cb_ref, wdn_ref, left_ref, gfin_ref, o_ref, st_ref,
                carry_ref, *, final_norm):
    tm = x_ref.shape[1]

    @pl.when(pl.program_id(1) == 0)
    def _():
        carry_ref[...] = jnp.zeros_like(carry_ref)
        carry_ref[6:8, :] = left_ref[0]

    x = x_ref[0]
    h = _norm_mod(x, g_ref[...], mod_ref[0, 4:5, :], mod_ref[0, 3:4, :]).astype(BF16)

    def conv_cols(lo):
        u = _dot(h, wup_ref[:, lo:lo + FFN_COLS])
        z = _causal_conv3(u, carry_ref[:, lo:lo + FFN_COLS], cw_ref[0:1, lo:lo + FFN_COLS],
                          cw_ref[1:2, lo:lo + FFN_COLS], cw_ref[2:3, lo:lo + FFN_COLS]) + cb_ref[:, lo:lo + FFN_COLS]
        carry_ref[:, lo:lo + FFN_COLS] = u[tm - SUBLANES:tm, :]
        st_ref[0, :, lo:lo + FFN_COLS] = u[tm - 2:tm, :]
        return z

    acc = jnp.zeros((tm, D_MODEL), F32)
    for c in range(0, D_FF, FFN_COLS):
        gate = conv_cols(c)
        val = conv_cols(D_FF + c)
        act = (gate * jax.nn.sigmoid(gate)) * val
        acc = acc + _dot(act.astype(BF16), wdn_ref[c:c + FFN_COLS, :])
    y = x + mod_ref[0, 5:6, :] * acc
    if final_norm:
        ms = jnp.mean(y * y, axis=-1, keepdims=True)
        y = y * lax.rsqrt(ms + NORM_EPS) * gfin_ref[...]
    o_ref[0] = y


def _ffn(x, mod, g, w_up, conv_w, conv_b, w_down, left, g_final, tm, final_norm):
    b, t, d = x.shape
    row3 = lambda bi, i: (bi, i, 0)
    per_b = lambda bi, i: (bi, 0, 0)
    const2 = lambda bi, i: (0, 0)
    kern = functools.partial(_ffn_kernel, final_norm=final_norm)
    return pl.pallas_call(
        kern,
        name="conv_ffn",
        out_shape=(jax.ShapeDtypeStruct((b, t, d), F32), jax.ShapeDtypeStruct((b, 2, 2 * D_FF), F32)),
        grid=(b, t // tm),
        in_specs=[pl.BlockSpec((1, tm, d), row3),
                  pl.BlockSpec((1, 6, d), per_b),
                  pl.BlockSpec((1, d), const2),
                  _wspec((d, 2 * D_FF)),
                  pl.BlockSpec((3, 2 * D_FF), const2),
                  pl.BlockSpec((1, 2 * D_FF), const2),
                  _wspec((D_FF, d)),
                  pl.BlockSpec((1, 2, 2 * D_FF), per_b),
                  pl.BlockSpec((1, d), const2)],
        out_specs=(pl.BlockSpec((1, tm, d), row3), pl.BlockSpec((1, 2, 2 * D_FF), per_b)),
        scratch_shapes=[pltpu.VMEM((SUBLANES, 2 * D_FF), F32)],
        compiler_params=_cparams(("parallel", "arbitrary")),
    )(x, mod, g, w_up, conv_w, conv_b, w_down, left, g_final)


def _softplus(z):
    return jnp.maximum(z, 0.0) + jnp.log1p(jnp.exp(-jnp.abs(z)))


def _rwkv_proj_kernel(x_ref, mod_ref, g_ref, sp_ref, mix_ref, wrkv_ref, w1_ref, w2_ref, w0_ref, a1_ref, a2_ref,
                      a0_ref, g1_ref, g2_ref, r_ref, k_ref, v_ref, w_ref, a_ref, gg_ref, sh_ref, carry_ref):
    tm = x_ref.shape[1]

    @pl.when(pl.program_id(1) == 0)
    def _():
        carry_ref[...] = jnp.zeros_like(carry_ref)
        carry_ref[7:8, :] = sp_ref[0]

    h = _norm_mod(x_ref[0], g_ref[...], mod_ref[0, 1:2, :], mod_ref[0, 0:1, :])
    (hs,) = _shifted(h, carry_ref[...], 1)
    carry_ref[...] = h[tm - SUBLANES:tm, :]
    sh_ref[0] = h[tm - 1:tm, :]
    xx = hs - h

    def mixed(j):
        return (h + xx * mix_ref[j:j + 1, :]).astype(BF16)

    r_ref[0] = _dot(mixed(0), wrkv_ref[0])
    k_ref[0] = _dot(mixed(2), wrkv_ref[1])
    v_ref[0] = _dot(mixed(3), wrkv_ref[2])
    lw = _dot(jnp.tanh(_dot(mixed(1), w1_ref[...])).astype(BF16), w2_ref[...])
    w_log = -_softplus(-(w0_ref[...] + lw)) - 0.5
    w_ref[0] = jnp.exp(-jnp.exp(w_log))
    a_ref[0] = jax.nn.sigmoid(a0_ref[...] + _dot(_dot(mixed(4), a1_ref[...]).astype(BF16), a2_ref[...]))
    gg_ref[0] = _dot(jax.nn.sigmoid(_dot(mixed(5), g1_ref[...])).astype(BF16), g2_ref[...])


def _rwkv_proj(x, mod, g, shift_prev, mix, w_rkv, w1, w2, w0, a1, a2, a0, g1, g2, tm):
    b, t, d = x.shape
    row3 = lambda bi, i: (bi, i, 0)
    per_b = lambda bi, i: (bi, 0, 0)
    const2 = lambda bi, i: (0, 0)
    big = jax.ShapeDtypeStruct((b, t, d), F32)
    lora = w1.shape[1]
    gl = g1.shape[1]
    return pl.pallas_call(
        _rwkv_proj_kernel,
        name="rwkv_proj",
        out_shape=(big,) * 6 + (jax.ShapeDtypeStruct((b, 1, d), F32),),
        grid=(b, t // tm),
        in_specs=[pl.BlockSpec((1, tm, d), row3),
                  pl.BlockSpec((1, 6, d), per_b),
                  pl.BlockSpec((1, d), const2),
                  pl.BlockSpec((1, 1, d), per_b),
                  pl.BlockSpec((6, d), const2),
                  _wspec((3, d, d)),
                  pl.BlockSpec((d, lora), const2),
                  pl.BlockSpec((lora, d), const2),
                  pl.BlockSpec((1, d), const2),
                  pl.BlockSpec((d, lora), const2),
                  pl.BlockSpec((lora, d), const2),
                  pl.BlockSpec((1, d), const2),
                  pl.BlockSpec((d, gl), const2),
                  pl.BlockSpec((gl, d), const2)],
        out_specs=(pl.BlockSpec((1, tm, d), row3),) * 6 + (pl.BlockSpec((1, 1, d), per_b),),
        scratch_shapes=[pltpu.VMEM((SUBLANES, d), F32)],
        compiler_params=_cparams(("parallel", "arbitrary")),
    )(x, mod, g, shift_prev, mix, w_rkv, w1, w2, w0, a1, a2, a0, g1, g2)


SCAN_STEPS = 32
def _rwkv_scan_kernel(r_ref, k_ref, v_ref, w_ref, a_ref, kk_ref, ka_ref, rk_ref, lg_ref, lb_ref, s0_ref,
                      y_ref, sfin_ref, s_ref, bc_ref):
    tc = r_ref.shape[0]
    n = RW_HEAD

    @pl.when(pl.program_id(1) == 0)
    def _():
        s_ref[...] = s0_ref[...]

    def unit_kk(t):
        kkr = k_ref[t] * kk_ref[...]
        nrm = jnp.sqrt(jnp.sum(kkr * kkr, axis=0, keepdims=True))
        return kkr / jnp.maximum(nrm, 1e-12)

    def step(t, carry, has_next):
        cum_prev, sa, kk = carry
        kt = k_ref[t]
        at = a_ref[t]
        rt = r_ref[t]
        vt = v_ref[t]
        cum = cum_prev * w_ref[t]
        inv = 1.0 / cum
        kmod = kt * (1.0 + (at - 1.0) * ka_ref[...])
        bc_ref[0] = (kk * at) * inv
        bc_ref[1] = kmod * inv
        bc_ref[2] = rt * cum
        if has_next:
            kk_next = unit_kk(t + 1)
            bc_ref[3] = -kk_next * cum
        y = jnp.zeros((n, LANES), F32)
        sa_next = jnp.zeros((n, LANES), F32)
        for j in range(n):
            sj = s_ref[j] + sa * bc_ref[0, j:j + 1, :] + vt * bc_ref[1, j:j + 1, :]
            s_ref[j] = sj
            y = y + sj * bc_ref[2, j:j + 1, :]
            if has_next:
                sa_next = sa_next + sj * bc_ref[3, j:j + 1, :]
        mu = jnp.mean(y, axis=0, keepdims=True)
        yc = y - mu
        var = jnp.mean(yc * yc, axis=0, keepdims=True)
        yn = (yc * lax.rsqrt(var + RW_GN_EPS)) * lg_ref[...] + lb_ref[...]
        bonus = jnp.sum(rt * kmod * rk_ref[...], axis=0, keepdims=True) * vt
        y_ref[t] = yn + bonus
        if has_next:
            return cum, sa_next, kk_next
        return cum

    kk0 = unit_kk(0)
    bc_ref[3] = -kk0
    sa0 = jnp.zeros((n, LANES), F32)
    for j in range(n):
        sa0 = sa0 + s_ref[j] * bc_ref[3, j:j + 1, :]
    carry = (jnp.ones((n, LANES), F32), sa0, kk0)
    carry = lax.fori_loop(0, tc - 1, lambda t, c: step(t, c, True), carry)
    bc_ref[0] = step(tc - 1, carry, False)
    for j in range(n):
        s_ref[j] = s_ref[j] * bc_ref[0, j:j + 1, :]
    sfin_ref[...] = s_ref[...]


def _rwkv_scan(r, k, v, w, a, kk_t, ka_t, rk_t, lg_t, lb_t, s0, tc):
    t, n, bh = r.shape
    seq = pl.BlockSpec((tc, n, LANES), lambda lb, ti: (ti, 0, lb))
    par = pl.BlockSpec((n, LANES), lambda lb, ti: (0, lb))
    st = pl.BlockSpec((n, n, LANES), lambda lb, ti: (0, 0, lb))
    return pl.pallas_call(
        _rwkv_scan_kernel,
        name="rwkv_scan",
        out_shape=(jax.ShapeDtypeStruct((t, n, bh), F32), jax.ShapeDtypeStruct((n, n, bh), F32)),
        grid=(bh // LANES, t // tc),
        in_specs=[seq] * 5 + [par] * 5 + [st],
        out_specs=(seq, st),
        scratch_shapes=[pltpu.VMEM((n, n, LANES), F32), pltpu.VMEM((5, n, LANES), F32)],
        compiler_params=_cparams(("parallel", "arbitrary")),
    )(r, k, v, w, a, kk_t, ka_t, rk_t, lg_t, lb_t, s0)


def _rwkv_out_kernel(y_ref, gg_ref, x_ref, mod_ref, wo_ref, o_ref):
    o_ref[0] = x_ref[0] + mod_ref[0, 2:3, :] * _dot((y_ref[0] * gg_ref[0]).astype(BF16), wo_ref[...])


def _rwkv_out(y, gg, x, mod, w_o, tm):
    b, t, d = x.shape
    row3 = lambda bi, i: (bi, i, 0)
    return pl.pallas_call(
        _rwkv_out_kernel,
        name="rwkv_out",
        out_shape=jax.ShapeDtypeStruct((b, t, d), F32),
        grid=(b, t // tm),
        in_specs=[pl.BlockSpec((1, tm, d), row3), pl.BlockSpec((1, tm, d), row3), pl.BlockSpec((1, tm, d), row3),
                  pl.BlockSpec((1, 6, d), lambda bi, i: (bi, 0, 0)),
                  _wspec((d, d))],
        out_specs=pl.BlockSpec((1, tm, d), row3),
        compiler_params=_cparams(("parallel", "parallel")),
    )(y, gg, x, mod, w_o)


def _pad_rows(a, rows):
    return jnp.pad(a, ((0, 0), (0, rows - a.shape[1]), (0, 0)))


def _trunk(x, mods, states, ffn_left, wts, past):
    b, t, d = x.shape
    tm = min(t, 512)
    tq = min(t, 256)
    g_norm = wts['g_norm']
    new_states = []
    new_ffn = []
    tables = _rope_tables(t, past)
    for i in range(DEPTH):
        mod = mods[i]
        kind = i % 3
        g_a = g_norm[i, 0][None, :]
        if kind == 0:
            j = i // 3
            past_k, past_v, past_ki = states[i]
            q, k, v, qi, kw, k4, v4 = _attn_proj(x, mod, g_a, wts['attn_w_main'][j], wts['attn_w_tail'][j],
                                                 tables, tm)
            ki = kw[:, :, :IDX_DIM]
            l_all = past + t
            if past:
                lp = -(-l_all // LANES) * LANES
                kvd = N_KV_HEADS * HEAD_DIM
                k_all = _pad_rows(jnp.concatenate([past_k.reshape(b, past, kvd), k], axis=1), lp)
                v_all = _pad_rows(jnp.concatenate([past_v.reshape(b, past, kvd), v], axis=1), lp)
                ki_all = _pad_rows(jnp.concatenate([past_ki, ki], axis=1), lp)
            else:
                k_all, v_all, ki_all = k, v, ki
            for row0 in range(0, t, tq):
                x = _attn_core(q, qi, kw, k_all, v_all, ki_all, x, mod, wts['attn_w_out'][j], past=past,
                               row0=row0, rows=tq, n_sel=min(TOPK_MAX, l_all // 4), tq=tq)
            new_states.append((k4, v4, ki))
        elif kind == 1:
            x, conv_st = _conv_mixer(x, mod, g_a, wts['sc_w_in'], wts['sc_conv_w'], wts['sc_w_out'],
                                     states[i][0], tm)
            new_states.append((conv_st,))
        else:
            shift_prev, wkv0 = states[i]
            r, k, v, w, a, gg, shift_new = _rwkv_proj(
                x, mod, g_a, shift_prev, wts['rw_mix'], wts['rw_w_rkv'], wts['rw_w1'], wts['rw_w2'], wts['rw_w0'],
                wts['rw_a1'], wts['rw_a2'], wts['rw_a0'], wts['rw_g1'], wts['rw_g2'], tm)
            bh = b * RW_HEADS

            def to_scan(z):
                return z.reshape(b, t, RW_HEADS, RW_HEAD).transpose(1, 3, 0, 2).reshape(t, RW_HEAD, bh)

            def head_tile(p):
                return jnp.tile(p.reshape(RW_HEADS, RW_HEAD).T, (1, b))

            s0 = wkv0.transpose(3, 2, 0, 1).reshape(RW_HEAD, RW_HEAD, bh)
            y, s_fin = _rwkv_scan(to_scan(r), to_scan(k), to_scan(v), to_scan(w), to_scan(a),
                                  head_tile(wts['rw_k_k']), head_tile(wts['rw_k_a']), head_tile(wts['rw_r_k']),
                                  head_tile(wts['rw_ln_g']), head_tile(wts['rw_ln_b']), s0, min(t, SCAN_STEPS))
            y = y.reshape(t, RW_HEAD, b, RW_HEADS).transpose(2, 0, 3, 1).reshape(b, t, d)
            wkv_new = s_fin.reshape(RW_HEAD, RW_HEAD, b, RW_HEADS).transpose(2, 3, 1, 0)
            x = _rwkv_out(y, gg, x, mod, wts['rw_w_o'], tm)
            new_states.append((shift_new, wkv_new))
        x, f_st = _ffn(x, mod, g_norm[i, 1][None, :], wts['ffn_w_up'][i], wts['ffn_conv_w'][i],
                       wts['ffn_conv_b'][i][None, :], wts['ffn_w_down'][i], ffn_left[i],
                       wts['g_final'][None, :], min(t, FFN_ROWS), i == DEPTH - 1)
        new_ffn.append(f_st)
    return x, new_states, jnp.stack(new_ffn)


def _pad_cols(w, cols):
    return jnp.pad(w, ((0, 0), (0, cols - w.shape[1])))


def _pad_rows2(w, rows):
    return jnp.pad(w, ((0, rows - w.shape[0]), (0, 0)))


def kernel(x_prompt, x_sample, c_prompt, c_sample, cache_k_0, cache_v_0, cache_kidx_0, state_conv_1, state_shift_2, state_wkv_2, cache_k_3, cache_v_3, cache_kidx_3, state_ffn_conv, w_mod, b_mod, g_norm, g_final, attn_w_in, attn_w_out, sc_w_in, sc_conv_w, sc_w_out, rw_mix, rw_w_rkv, rw_w_o, rw_w0, rw_w1, rw_w2, rw_a0, rw_a1, rw_a2, rw_g1, rw_g2, rw_k_k, rw_k_a, rw_r_k, rw_ln_g, rw_ln_b, ffn_w_up, ffn_conv_w, ffn_conv_b, ffn_w_down):
    d = D_MODEL
    bp = x_prompt.shape[0]
    bs = x_sample.shape[0]
    dt = x_prompt.dtype

    wts = _prep_weights(g_norm, g_final, attn_w_in, attn_w_out, sc_w_in, sc_conv_w, sc_w_out, rw_mix, rw_w_rkv,
                        rw_w_o, rw_w0, rw_w1, rw_w2, rw_a0, rw_a1, rw_a2, rw_g1, rw_g2, rw_k_k, rw_k_a, rw_r_k,
                        rw_ln_g, rw_ln_b, ffn_w_up, ffn_conv_w, ffn_conv_b, ffn_w_down)

    mods = _mod_all(jnp.concatenate([c_prompt, c_sample], axis=0), w_mod, b_mod)
    mods_p = mods[:, :bp].reshape(DEPTH, bp, 6, d)
    mods_s = mods[:, bp:].reshape(DEPTH, bs, 6, d)

    ffn0 = jnp.zeros((DEPTH, bp, 2, 2 * D_FF), dt)
    y_p, st_p, ffn_p = _trunk(x_prompt, mods_p, _empty_states(bp, dt), ffn0, wts, 0)
    sample_states = [(cache_k_0, cache_v_0, cache_kidx_0), (state_conv_1,), (state_shift_2, state_wkv_2),
                     (cache_k_3, cache_v_3, cache_kidx_3)]
    y_s, st_s, ffn_s = _trunk(x_sample, mods_s, sample_states, state_ffn_conv, wts, cache_k_0.shape[1])
    (k0_p, v0_p, ki0_p), (conv1_p,), (shift2_p, wkv2_p), (k3_p, v3_p, ki3_p) = st_p
    (k0_s, v0_s, ki0_s), (conv1_s,), (shift2_s, wkv2_s), (k3_s, v3_s, ki3_s) = st_s
    return (y_p, y_s,
            k0_p, v0_p, ki0_p, conv1_p, shift2_p, wkv2_p, k3_p, v3_p, ki3_p, ffn_p,
            k0_s, v0_s, ki0_s, conv1_s, shift2_s, wkv2_s, k3_s, v3_s, ki3_s, ffn_s)


def _empty_states(b, dt):
    d = D_MODEL
    kv = (jnp.zeros((b, 0, N_KV_HEADS, HEAD_DIM), dt), jnp.zeros((b, 0, N_KV_HEADS, HEAD_DIM), dt),
          jnp.zeros((b, 0, IDX_DIM), dt))
    return [kv, (jnp.zeros((b, 2, d), dt),),
            (jnp.zeros((b, 1, d), dt), jnp.zeros((b, RW_HEADS, RW_HEAD, RW_HEAD), dt)), kv]


def _prep_weights(g_norm, g_final, attn_w_in, attn_w_out, sc_w_in, sc_conv_w, sc_w_out, rw_mix, rw_w_rkv, rw_w_o,
                  rw_w0, rw_w1, rw_w2, rw_a0, rw_a1, rw_a2, rw_g1, rw_g2, rw_k_k, rw_k_a, rw_r_k, rw_ln_g, rw_ln_b,
                  ffn_w_up, ffn_conv_w, ffn_conv_b, ffn_w_down):
    return {
        'g_norm': g_norm, 'g_final': g_final,
        'attn_w_main': attn_w_in[:, :, :QI_END].astype(BF16),
        'attn_w_tail': jnp.pad(attn_w_in[:, :, QI_END:], ((0, 0), (0, 0), (0, LANES - (WI_END - QI_END)))).astype(BF16),
        'attn_w_out': attn_w_out.astype(BF16),
        'sc_w_in': sc_w_in.astype(BF16), 'sc_conv_w': sc_conv_w, 'sc_w_out': sc_w_out.astype(BF16),
        'rw_mix': rw_mix, 'rw_w_rkv': rw_w_rkv.astype(BF16), 'rw_w_o': rw_w_o.astype(BF16),
        'rw_w0': rw_w0[None, :], 'rw_w1': _pad_cols(rw_w1, LANES).astype(BF16),
        'rw_w2': _pad_rows2(rw_w2, LANES).astype(BF16),
        'rw_a0': rw_a0[None, :], 'rw_a1': _pad_cols(rw_a1, LANES).astype(BF16),
        'rw_a2': _pad_rows2(rw_a2, LANES).astype(BF16),
        'rw_g1': rw_g1.astype(BF16), 'rw_g2': rw_g2.astype(BF16),
        'rw_k_k': rw_k_k, 'rw_k_a': rw_k_a, 'rw_r_k': rw_r_k, 'rw_ln_g': rw_ln_g, 'rw_ln_b': rw_ln_b,
        'ffn_w_up': ffn_w_up.astype(BF16), 'ffn_conv_w': ffn_conv_w, 'ffn_conv_b': ffn_conv_b,
        'ffn_w_down': ffn_w_down.astype(BF16),
    }
```

```python
import functools

import jax
import jax.numpy as jnp
from jax import lax
from jax.experimental import pallas as pl
from jax.experimental.pallas import tpu as pltpu

D_MODEL = 1024
DEPTH = 4
CHUNK = 64
N_HEADS = 8
HEAD_DIM = 128
N_KV_HEADS = 2
GROUP = N_HEADS // N_KV_HEADS
IDX_HEADS = 8
IDX_DIM = 64
TOPK_MAX = 256
ROPE_THETA = 10000.0
RW_HEAD = 64
RW_HEADS = D_MODEL // RW_HEAD
RW_GN_EPS = 64e-5
D_FF = 2 * D_MODEL
NORM_EPS = 1e-6

LANES = 128
SUBLANES = 8
VMEM_LIMIT = 52 * 1024 * 1024
INT_MIN = -2 ** 31
F32_TINY = 1.1754943508222875e-38

F32 = jnp.float32
BF16 = jnp.bfloat16

Q_END = N_HEADS * HEAD_DIM
K_END = Q_END + N_KV_HEADS * HEAD_DIM
V_END = K_END + N_KV_HEADS * HEAD_DIM
QI_END = V_END + IDX_HEADS * IDX_DIM
KI_END = QI_END + IDX_DIM
WI_END = KI_END + IDX_HEADS
Q_SCALE = HEAD_DIM ** -0.5 * 1.4426950408889634


def _cparams(sem):
    return pltpu.CompilerParams(dimension_semantics=sem, vmem_limit_bytes=VMEM_LIMIT)


def _wspec(shape):
    return pl.BlockSpec(shape, lambda bi, i: (0,) * len(shape), pipeline_mode=pl.Buffered(1))


def _dot(a, b):
    return jnp.dot(a, b, preferred_element_type=F32)


def _dot_nt(a, b):
    return lax.dot_general(a, b, (((1,), (1,)), ((), ())), preferred_element_type=F32)


def _dot_nt_split(a, b):
    half = a.shape[0] // 2
    return jnp.concatenate([_dot_nt(a[:half], b), _dot_nt(a[half:], b)], axis=0)


def _dot_ksplit(a, b):
    half = a.shape[1] // 2
    return _dot(a[:, :half], b[:half]) + _dot(a[:, half:], b[half:])


def _norm_mod(x, g, scale, shift):
    ms = jnp.mean(x * x, axis=-1, keepdims=True)
    return (x * lax.rsqrt(ms + NORM_EPS) * g) * (1.0 + scale) + shift


def _shifted(u, hist, n):
    tm, c = u.shape
    tiles = jnp.concatenate([hist, u], axis=0).reshape(tm // SUBLANES + 1, SUBLANES, c)
    row = lax.broadcasted_iota(jnp.int32, (1, SUBLANES, 1), 1)
    outs = []
    for s in range(n, 0, -1):
        rot = pltpu.roll(tiles, s, 1)
        outs.append(jnp.where(row < s, rot[:-1], rot[1:]).reshape(tm, c))
    return outs


def _causal_conv3(u, hist, w0, w1, w2):
    s2, s1 = _shifted(u, hist, 2)
    return w0 * s2 + w1 * s1 + w2 * u


def _mod_kernel(c_ref, w_ref, b_ref, o_ref):
    o_ref[0] = _dot(c_ref[...].astype(BF16), w_ref[0].astype(BF16)) + b_ref[0]


def _mod_all(c_all, w_mod, b_mod):
    nb = c_all.shape[0]
    d = D_MODEL
    return pl.pallas_call(
        _mod_kernel,
        name="adaln_mod",
        out_shape=jax.ShapeDtypeStruct((DEPTH, nb, 6 * d), F32),
        grid=(DEPTH, 6),
        in_specs=[pl.BlockSpec((nb, d), lambda l, n: (0, 0)),
                  pl.BlockSpec((1, d, d), lambda l, n: (l, 0, n)),
                  pl.BlockSpec((1, 1, d), lambda l, n: (l, 0, n))],
        out_specs=pl.BlockSpec((1, nb, d), lambda l, n: (l, 0, n)),
        compiler_params=_cparams(("parallel", "parallel")),
    )(c_all, w_mod, b_mod.reshape(DEPTH, 1, 6 * d))


def _rope128(seg, cos, sin_signed):
    return seg * cos + pltpu.roll(seg, HEAD_DIM // 2, 1) * sin_signed


def _rope64(seg, cos, sin_lo, sin_hi):
    return seg * cos + pltpu.roll(seg, LANES - IDX_DIM // 2, 1) * sin_lo + pltpu.roll(seg, IDX_DIM // 2, 1) * sin_hi


def _attn_proj_kernel(x_ref, mod_ref, g_ref, wm_ref, wt_ref, c128_ref, s128_ref, c64_ref, s64lo_ref,
                      s64hi_ref, ct_ref, stlo_ref, sthi_ref, q_ref, k_ref, v_ref, qi_ref, kw_ref, k4_ref, v4_ref):
    h = _norm_mod(x_ref[0], g_ref[...], mod_ref[0, 1:2, :], mod_ref[0, 0:1, :]).astype(BF16)
    pm = _dot(h, wm_ref[...])
    c128 = c128_ref[...]
    s128 = s128_ref[...]
    for hh in range(N_HEADS):
        lo = hh * HEAD_DIM
        q_ref[0, :, lo:lo + HEAD_DIM] = _rope128(pm[:, lo:lo + HEAD_DIM], c128, s128) * Q_SCALE
    for hh in range(N_KV_HEADS):
        lo = hh * HEAD_DIM
        kh = _rope128(pm[:, Q_END + lo:Q_END + lo + HEAD_DIM], c128, s128)
        k_ref[0, :, lo:lo + HEAD_DIM] = kh
        k4_ref[0, :, hh, :] = kh
        v4_ref[0, :, hh, :] = pm[:, K_END + lo:K_END + lo + HEAD_DIM]
    v_ref[0] = pm[:, K_END:V_END]
    c64 = c64_ref[...]
    s64lo = s64lo_ref[...]
    s64hi = s64hi_ref[...]
    for cc in range(IDX_HEADS * IDX_DIM // LANES):
        lo = cc * LANES
        qi_ref[0, :, lo:lo + LANES] = _rope64(pm[:, V_END + lo:V_END + lo + LANES], c64, s64lo, s64hi)
    pt = _dot(h, wt_ref[...])
    kw_ref[0] = _rope64(pt, ct_ref[...], stlo_ref[...], sthi_ref[...])


def _rope_tables(t_len, past):
    pos = jnp.arange(past, past + t_len, dtype=jnp.int32).astype(F32)[:, None]

    def cs(d):
        half = d // 2
        inv = ROPE_THETA ** (-2.0 * jnp.arange(half, dtype=F32) / d)
        ang = pos * inv[None, :]
        return jnp.cos(ang), jnp.sin(ang)

    c, s = cs(HEAD_DIM)
    c128 = jnp.concatenate([c, c], axis=1)
    s128 = jnp.concatenate([-s, s], axis=1)
    c, s = cs(IDX_DIM)
    z = jnp.zeros_like(s)
    c64 = jnp.concatenate([c, c, c, c], axis=1)
    s64lo = jnp.concatenate([-s, z, -s, z], axis=1)
    s64hi = jnp.concatenate([z, s, z, s], axis=1)
    wscale = jnp.full((t_len, LANES - IDX_DIM), IDX_HEADS ** -0.5, F32)
    ct = jnp.concatenate([c, c, wscale], axis=1)
    stlo = jnp.concatenate([-s, z, z, z], axis=1)
    sthi = jnp.concatenate([z, s, z, z], axis=1)
    return c128, s128, c64, s64lo, s64hi, ct, stlo, sthi


def _attn_proj(x, mod, g, w_main, w_tail, tables, tm):
    b, t, d = x.shape
    tab_spec = pl.BlockSpec((tm, LANES), lambda bi, i: (i, 0))
    const2 = lambda bi, i: (0, 0)
    row3 = lambda bi, i: (bi, i, 0)
    return pl.pallas_call(
        _attn_proj_kernel,
        name="attn_proj",
        out_shape=(jax.ShapeDtypeStruct((b, t, Q_END), F32),
                   jax.ShapeDtypeStruct((b, t, K_END - Q_END), F32),
                   jax.ShapeDtypeStruct((b, t, V_END - K_END), F32),
                   jax.ShapeDtypeStruct((b, t, QI_END - V_END), F32),
                   jax.ShapeDtypeStruct((b, t, LANES), F32),
                   jax.ShapeDtypeStruct((b, t, N_KV_HEADS, HEAD_DIM), F32),
                   jax.ShapeDtypeStruct((b, t, N_KV_HEADS, HEAD_DIM), F32)),
        grid=(b, t // tm),
        in_specs=[pl.BlockSpec((1, tm, d), row3),
                  pl.BlockSpec((1, 6, d), lambda bi, i: (bi, 0, 0)),
                  pl.BlockSpec((1, d), const2),
                  _wspec((d, QI_END)),
                  _wspec((d, LANES))] + [tab_spec] * 8,
        out_specs=(pl.BlockSpec((1, tm, Q_END), row3),
                   pl.BlockSpec((1, tm, K_END - Q_END), row3),
                   pl.BlockSpec((1, tm, V_END - K_END), row3),
                   pl.BlockSpec((1, tm, QI_END - V_END), row3),
                   pl.BlockSpec((1, tm, LANES), row3),
                   pl.BlockSpec((1, tm, N_KV_HEADS, HEAD_DIM), lambda bi, i: (bi, i, 0, 0)),
                   pl.BlockSpec((1, tm, N_KV_HEADS, HEAD_DIM), lambda bi, i: (bi, i, 0, 0))),
        compiler_params=_cparams(("parallel", "parallel")),
    )(x, mod, g, w_main, w_tail, *tables)


def _col_reduce(x, op):
    rows, cols = x.shape
    slab = 8 * SUBLANES
    if rows % slab == 0 and rows > slab:
        x = op(x.reshape(rows // slab, slab, cols), axis=0)
    return op(x, axis=0, keepdims=True)


def _attn_core_kernel(q_ref, qi_ref, kw_ref, k_ref, v_ref, ki_ref, x_ref, mod_ref, wo_ref, o_ref,
                      *, pos0, l_true, n_sel):
    tq = q_ref.shape[1]
    lp = k_ref.shape[1]
    i = pl.program_id(1)
    qpos = pos0 + i * tq + lax.broadcasted_iota(jnp.int32, (1, tq), 1)
    kidx = lax.broadcasted_iota(jnp.int32, (lp, tq), 0)
    adm = (lax.shift_right_logical(kidx, 6) <= lax.shift_right_logical(qpos, 6)) & (kidx < l_true)
    sel = adm if l_true <= n_sel else _topk_mask(qi_ref, kw_ref, ki_ref, adm, kidx, n_sel)

    outs = []
    for kv in range(N_KV_HEADS):
        kk = k_ref[0, :, kv * HEAD_DIM:(kv + 1) * HEAD_DIM].astype(BF16)
        vt = v_ref[0, :, kv * HEAD_DIM:(kv + 1) * HEAD_DIM].T.astype(BF16)
        for gi in range(GROUP):
            hq = kv * GROUP + gi
            qh = q_ref[0, :, hq * HEAD_DIM:(hq + 1) * HEAD_DIM].astype(BF16)
            s = jnp.where(sel, _dot_nt_split(kk, qh), -jnp.inf)
            m = _col_reduce(s, jnp.max)
            p = jnp.exp2(s - m)
            l = _col_reduce(p, jnp.sum)
            outs.append(_dot_ksplit(vt, p.astype(BF16)) / l)
    o = jnp.concatenate(outs, axis=0).T.astype(BF16)
    o_ref[0] = x_ref[0] + mod_ref[0, 2:3, :] * _dot(o, wo_ref[...])


def _topk_mask(qi_ref, kw_ref, ki_ref, adm, kidx, n_sel):
    lp, tq = kidx.shape
    kib = ki_ref[0].astype(BF16)
    wit = kw_ref[0].T
    score = jnp.zeros((lp, tq), F32)
    for hh in range(IDX_HEADS):
        qh = qi_ref[0, :, hh * IDX_DIM:(hh + 1) * IDX_DIM].astype(BF16)
        score = score + wit[IDX_DIM + hh:IDX_DIM + hh + 1, :] * jnp.maximum(_dot_nt_split(kib, qh), 0.0)

    score = jnp.where(jnp.abs(score) < F32_TINY, 0.0, score)
    bits = lax.bitcast_convert_type(score, jnp.int32)
    key = bits ^ ((bits >> 31) & jnp.int32(0x7FFFFFFF))
    key = jnp.where(adm, key, jnp.int32(INT_MIN))
    kf = jnp.float32(n_sel)

    def thr_body(it, t):
        cand = t + lax.shift_left(jnp.int32(1), 31 - it)
        cnt = _col_reduce(jnp.where(key >= cand, 1.0, 0.0), jnp.sum)
        return jnp.where(cnt >= kf, cand, t)

    thr = lax.fori_loop(0, 32, thr_body, jnp.full((1, tq), INT_MIN, jnp.int32))
    gt = key > thr
    eq = (key == thr) & adm
    need = kf - _col_reduce(jnp.where(gt, 1.0, 0.0), jnp.sum)
    eqf = jnp.where(eq, 1.0, 0.0)
    excess = _col_reduce(eqf, jnp.sum) - need

    def tie_search():
        nbits = lp.bit_length()

        def body(it, m):
            cand = m + lax.shift_left(jnp.int32(1), nbits - 1 - it)
            f = _col_reduce(jnp.where(kidx < cand, eqf, 0.0), jnp.sum)
            return jnp.where(f < need, cand, m)

        return lax.fori_loop(0, nbits, body, jnp.zeros((1, tq), jnp.int32)) + 1

    jstar = lax.cond(jnp.max(excess) > 0.0, tie_search, lambda: jnp.full((1, tq), lp, jnp.int32))
    return gt | (eq & (kidx < jstar))


def _attn_core(q, qi, kw, k_all, v_all, ki_all, x, mod, w_out, *, past, row0, rows, n_sel, tq):
    b, t, d = x.shape
    l_true = past + row0 + rows
    lp = -(-l_true // LANES) * LANES
    blk0 = row0 // tq
    row3 = lambda bi, i: (bi, blk0 + i, 0)
    per_b = lambda bi, i: (bi, 0, 0)
    kern = functools.partial(_attn_core_kernel, pos0=past + row0, l_true=l_true, n_sel=n_sel)
    return pl.pallas_call(
        kern,
        name=f"attn_core_l{lp}",
        out_shape=jax.ShapeDtypeStruct((b, t, d), F32),
        input_output_aliases={6: 0},
        grid=(b, rows // tq),
        in_specs=[pl.BlockSpec((1, tq, Q_END), row3),
                  pl.BlockSpec((1, tq, QI_END - V_END), row3),
                  pl.BlockSpec((1, tq, LANES), row3),
                  pl.BlockSpec((1, lp, K_END - Q_END), per_b),
                  pl.BlockSpec((1, lp, V_END - K_END), per_b),
                  pl.BlockSpec((1, lp, IDX_DIM), per_b),
                  pl.BlockSpec((1, tq, d), row3),
                  pl.BlockSpec((1, 6, d), per_b),
                  _wspec((N_HEADS * HEAD_DIM, d))],
        out_specs=pl.BlockSpec((1, tq, d), row3),
        compiler_params=_cparams(("parallel", "parallel")),
    )(q, qi, kw, k_all, v_all, ki_all, x, mod, w_out)


def _conv_mixer_kernel(x_ref, mod_ref, g_ref, win_ref, cw_ref, wout_ref, left_ref, o_ref, st_ref, carry_ref):
    d = D_MODEL
    tm = x_ref.shape[1]

    @pl.when(pl.program_id(1) == 0)
    def _():
        carry_ref[...] = jnp.zeros_like(carry_ref)
        carry_ref[6:8, :] = left_ref[0]

    x = x_ref[0]
    h = _norm_mod(x, g_ref[...], mod_ref[0, 1:2, :], mod_ref[0, 0:1, :]).astype(BF16)
    gb = _dot(h, win_ref[:, 0:d])
    gc = _dot(h, win_ref[:, d:2 * d])
    u = gc * _dot(h, win_ref[:, 2 * d:3 * d])
    y = _causal_conv3(u, carry_ref[...], cw_ref[0:1, :], cw_ref[1:2, :], cw_ref[2:3, :])
    carry_ref[...] = u[tm - SUBLANES:tm, :]
    st_ref[0] = u[tm - 2:tm, :]
    o_ref[0] = x + mod_ref[0, 2:3, :] * _dot((gb * y).astype(BF16), wout_ref[...])


def _conv_mixer(x, mod, g, w_in, conv_w, w_out, left, tm):
    b, t, d = x.shape
    row3 = lambda bi, i: (bi, i, 0)
    per_b = lambda bi, i: (bi, 0, 0)
    const2 = lambda bi, i: (0, 0)
    return pl.pallas_call(
        _conv_mixer_kernel,
        name="conv_mixer",
        out_shape=(jax.ShapeDtypeStruct((b, t, d), F32), jax.ShapeDtypeStruct((b, 2, d), F32)),
        grid=(b, t // tm),
        in_specs=[pl.BlockSpec((1, tm, d), row3),
                  pl.BlockSpec((1, 6, d), per_b),
                  pl.BlockSpec((1, d), const2),
                  _wspec((d, 3 * d)),
                  pl.BlockSpec((3, d), const2),
                  _wspec((d, d)),
                  pl.BlockSpec((1, 2, d), per_b)],
        out_specs=(pl.BlockSpec((1, tm, d), row3), pl.BlockSpec((1, 2, d), per_b)),
        scratch_shapes=[pltpu.VMEM((SUBLANES, d), F32)],
        compiler_params=_cparams(("parallel", "arbitrary")),
    )(x, mod, g, w_in, conv_w, w_out, left)


FFN_COLS = 512
FFN_ROWS = 1024


def _ffn_kernel(x_ref, mod_ref, g_ref, wup_ref, cw_ref, cb_ref, wdn_ref, left_ref, gfin_ref, o_ref, st_ref,
                carry_ref, *, final_norm):
    tm = x_ref.shape[1]

    @pl.when(pl.program_id(1) == 0)
    def _():
        carry_ref[...] = jnp.zeros_like(carry_ref)
        carry_ref[6:8, :] = left_ref[0]

    x = x_ref[0]
    h = _norm_mod(x, g_ref[...], mod_ref[0, 4:5, :], mod_ref[0, 3:4, :]).astype(BF16)

    def conv_cols(lo):
        u = _dot(h, wup_ref[:, lo:lo + FFN_COLS])
        z = _causal_conv3(u, carry_ref[:, lo:lo + FFN_COLS], cw_ref[0:1, lo:lo + FFN_COLS],
                          cw_ref[1:2, lo:lo + FFN_COLS], cw_ref[2:3, lo:lo + FFN_COLS]) + cb_ref[:, lo:lo + FFN_COLS]
        carry_ref[:, lo:lo + FFN_COLS] = u[tm - SUBLANES:tm, :]
        st_ref[0, :, lo:lo + FFN_COLS] = u[tm - 2:tm, :]
        return z

    acc = jnp.zeros((tm, D_MODEL), F32)
    for c in range(0, D_FF, FFN_COLS):
        gate = conv_cols(c)
        val = conv_cols(D_FF + c)
        act = (gate * jax.nn.sigmoid(gate)) * val
        acc = acc + _dot(act.astype(BF16), wdn_ref[c:c + FFN_COLS, :])
    y = x + mod_ref[0, 5:6, :] * acc
    if final_norm:
        ms = jnp.mean(y * y, axis=-1, keepdims=True)
        y = y * lax.rsqrt(ms + NORM_EPS) * gfin_ref[...]
    o_ref[0] = y


def _ffn(x, mod, g, w_up, conv_w, conv_b, w_down, left, g_final, tm, final_norm):
    b, t, d = x.shape
    row3 = lambda bi, i: (bi, i, 0)
    per_b = lambda bi, i: (bi, 0, 0)
    const2 = lambda bi, i: (0, 0)
    kern = functools.partial(_ffn_kernel, final_norm=final_norm)
    return pl.pallas_call(
        kern,
        name="conv_ffn",
        out_shape=(jax.ShapeDtypeStruct((b, t, d), F32), jax.ShapeDtypeStruct((b, 2, 2 * D_FF), F32)),
        grid=(b, t // tm),
        in_specs=[pl.BlockSpec((1, tm, d), row3),
                  pl.BlockSpec((1, 6, d), per_b),
                  pl.BlockSpec((1, d), const2),
                  _wspec((d, 2 * D_FF)),
                  pl.BlockSpec((3, 2 * D_FF), const2),
                  pl.BlockSpec((1, 2 * D_FF), const2),
                  _wspec((D_FF, d)),
                  pl.BlockSpec((1, 2, 2 * D_FF), per_b),
                  pl.BlockSpec((1, d), const2)],
        out_specs=(pl.BlockSpec((1, tm, d), row3), pl.BlockSpec((1, 2, 2 * D_FF), per_b)),
        scratch_shapes=[pltpu.VMEM((SUBLANES, 2 * D_FF), F32)],
        compiler_params=_cparams(("parallel", "arbitrary")),
    )(x, mod, g, w_up, conv_w, conv_b, w_down, left, g_final)


def _softplus(z):
    return jnp.maximum(z, 0.0) + jnp.log1p(jnp.exp(-jnp.abs(z)))


def _rwkv_proj_kernel(x_ref, mod_ref, g_ref, sp_ref, mix_ref, wrkv_ref, w1_ref, w2_ref, w0_ref, a1_ref, a2_ref,
                      a0_ref, g1_ref, g2_ref, r_ref, k_ref, v_ref, w_ref, a_ref, gg_ref, sh_ref, carry_ref):
    tm = x_ref.shape[1]

    @pl.when(pl.program_id(1) == 0)
    def _():
        carry_ref[...] = jnp.zeros_like(carry_ref)
        carry_ref[7:8, :] = sp_ref[0]

    h = _norm_mod(x_ref[0], g_ref[...], mod_ref[0, 1:2, :], mod_ref[0, 0:1, :])
    (hs,) = _shifted(h, carry_ref[...], 1)
    carry_ref[...] = h[tm - SUBLANES:tm, :]
    sh_ref[0] = h[tm - 1:tm, :]
    xx = hs - h

    def mixed(j):
        return (h + xx * mix_ref[j:j + 1, :]).astype(BF16)

    r_ref[0] = _dot(mixed(0), wrkv_ref[0])
    k_ref[0] = _dot(mixed(2), wrkv_ref[1])
    v_ref[0] = _dot(mixed(3), wrkv_ref[2])
    lw = _dot(jnp.tanh(_dot(mixed(1), w1_ref[...])).astype(BF16), w2_ref[...])
    w_log = -_softplus(-(w0_ref[...] + lw)) - 0.5
    w_ref[0] = jnp.exp(-jnp.exp(w_log))
    a_ref[0] = jax.nn.sigmoid(a0_ref[...] + _dot(_dot(mixed(4), a1_ref[...]).astype(BF16), a2_ref[...]))
    gg_ref[0] = _dot(jax.nn.sigmoid(_dot(mixed(5), g1_ref[...])).astype(BF16), g2_ref[...])


def _rwkv_proj(x, mod, g, shift_prev, mix, w_rkv, w1, w2, w0, a1, a2, a0, g1, g2, tm):
    b, t, d = x.shape
    row3 = lambda bi, i: (bi, i, 0)
    per_b = lambda bi, i: (bi, 0, 0)
    const2 = lambda bi, i: (0, 0)
    big = jax.ShapeDtypeStruct((b, t, d), F32)
    lora = w1.shape[1]
    gl = g1.shape[1]
    return pl.pallas_call(
        _rwkv_proj_kernel,
        name="rwkv_proj",
        out_shape=(big,) * 6 + (jax.ShapeDtypeStruct((b, 1, d), F32),),
        grid=(b, t // tm),
        in_specs=[pl.BlockSpec((1, tm, d), row3),
                  pl.BlockSpec((1, 6, d), per_b),
                  pl.BlockSpec((1, d), const2),
                  pl.BlockSpec((1, 1, d), per_b),
                  pl.BlockSpec((6, d), const2),
                  _wspec((3, d, d)),
                  pl.BlockSpec((d, lora), const2),
                  pl.BlockSpec((lora, d), const2),
                  pl.BlockSpec((1, d), const2),
                  pl.BlockSpec((d, lora), const2),
                  pl.BlockSpec((lora, d), const2),
                  pl.BlockSpec((1, d), const2),
                  pl.BlockSpec((d, gl), const2),
                  pl.BlockSpec((gl, d), const2)],
        out_specs=(pl.BlockSpec((1, tm, d), row3),) * 6 + (pl.BlockSpec((1, 1, d), per_b),),
        scratch_shapes=[pltpu.VMEM((SUBLANES, d), F32)],
        compiler_params=_cparams(("parallel", "arbitrary")),
    )(x, mod, g, shift_prev, mix, w_rkv, w1, w2, w0, a1, a2, a0, g1, g2)


SCAN_STEPS = 32
def _rwkv_scan_kernel(r_ref, k_ref, v_ref, w_ref, a_ref, kk_ref, ka_ref, rk_ref, lg_ref, lb_ref, s0_ref,
                      y_ref, sfin_ref, s_ref, bc_ref):
    tc = r_ref.shape[0]
    n = RW_HEAD

    @pl.when(pl.program_id(1) == 0)
    def _():
        s_ref[...] = s0_ref[...]

    def unit_kk(t):
        kkr = k_ref[t] * kk_ref[...]
        nrm = jnp.sqrt(jnp.sum(kkr * kkr, axis=0, keepdims=True))
        return kkr / jnp.maximum(nrm, 1e-12)

    def step(t, carry, has_next):
        cum_prev, sa, kk = carry
        kt = k_ref[t]
        at = a_ref[t]
        rt = r_ref[t]
        vt = v_ref[t]
        cum = cum_prev * w_ref[t]
        inv = 1.0 / cum
        kmod = kt * (1.0 + (at - 1.0) * ka_ref[...])
        bc_ref[0] = (kk * at) * inv
        bc_ref[1] = kmod * inv
        bc_ref[2] = rt * cum
        if has_next:
            kk_next = unit_kk(t + 1)
            bc_ref[3] = -kk_next * cum
        y = jnp.zeros((n, LANES), F32)
        sa_next = jnp.zeros((n, LANES), F32)
        for j in range(n):
            sj = s_ref[j] + sa * bc_ref[0, j:j + 1, :] + vt * bc_ref[1, j:j + 1, :]
            s_ref[j] = sj
            y = y + sj * bc_ref[2, j:j + 1, :]
            if has_next:
                sa_next = sa_next + sj * bc_ref[3, j:j + 1, :]
        mu = jnp.mean(y, axis=0, keepdims=True)
        yc = y - mu
        var = jnp.mean(yc * yc, axis=0, keepdims=True)
        yn = (yc * lax.rsqrt(var + RW_GN_EPS)) * lg_ref[...] + lb_ref[...]
        bonus = jnp.sum(rt * kmod * rk_ref[...], axis=0, keepdims=True) * vt
        y_ref[t] = yn + bonus
        if has_next:
            return cum, sa_next, kk_next
        return cum

    kk0 = unit_kk(0)
    bc_ref[3] = -kk0
    sa0 = jnp.zeros((n, LANES), F32)
    for j in range(n):
        sa0 = sa0 + s_ref[j] * bc_ref[3, j:j + 1, :]
    carry = (jnp.ones((n, LANES), F32), sa0, kk0)
    carry = lax.fori_loop(0, tc - 1, lambda t, c: step(t, c, True), carry)
    bc_ref[0] = step(tc - 1, carry, False)
    for j in range(n):
        s_ref[j] = s_ref[j] * bc_ref[0, j:j + 1, :]
    sfin_ref[...] = s_ref[...]


def _rwkv_scan(r, k, v, w, a, kk_t, ka_t, rk_t, lg_t, lb_t, s0, tc):
    t, n, bh = r.shape
    seq = pl.BlockSpec((tc, n, LANES), lambda lb, ti: (ti, 0, lb))
    par = pl.BlockSpec((n, LANES), lambda lb, ti: (0, lb))
    st = pl.BlockSpec((n, n, LANES), lambda lb, ti: (0, 0, lb))
    return pl.pallas_call(
        _rwkv_scan_kernel,
        name="rwkv_scan",
        out_shape=(jax.ShapeDtypeStruct((t, n, bh), F32), jax.ShapeDtypeStruct((n, n, bh), F32)),
        grid=(bh // LANES, t // tc),
        in_specs=[seq] * 5 + [par] * 5 + [st],
        out_specs=(seq, st),
        scratch_shapes=[pltpu.VMEM((n, n, LANES), F32), pltpu.VMEM((5, n, LANES), F32)],
        compiler_params=_cparams(("parallel", "arbitrary")),
    )(r, k, v, w, a, kk_t, ka_t, rk_t, lg_t, lb_t, s0)


def _rwkv_out_kernel(y_ref, gg_ref, x_ref, mod_ref, wo_ref, o_ref):
    o_ref[0] = x_ref[0] + mod_ref[0, 2:3, :] * _dot((y_ref[0] * gg_ref[0]).astype(BF16), wo_ref[...])


def _rwkv_out(y, gg, x, mod, w_o, tm):
    b, t, d = x.shape
    row3 = lambda bi, i: (bi, i, 0)
    return pl.pallas_call(
        _rwkv_out_kernel,
        name="rwkv_out",
        out_shape=jax.ShapeDtypeStruct((b, t, d), F32),
        grid=(b, t // tm),
        in_specs=[pl.BlockSpec((1, tm, d), row3), pl.BlockSpec((1, tm, d), row3), pl.BlockSpec((1, tm, d), row3),
                  pl.BlockSpec((1, 6, d), lambda bi, i: (bi, 0, 0)),
                  _wspec((d, d))],
        out_specs=pl.BlockSpec((1, tm, d), row3),
        compiler_params=_cparams(("parallel", "parallel")),
    )(y, gg, x, mod, w_o)


def _pad_rows(a, rows):
    return jnp.pad(a, ((0, 0), (0, rows - a.shape[1]), (0, 0)))


def _trunk(x, mods, states, ffn_left, wts, past):
    b, t, d = x.shape
    tm = min(t, 512)
    tq = min(t, 256)
    g_norm = wts['g_norm']
    new_states = []
    new_ffn = []
    tables = _rope_tables(t, past)
    for i in range(DEPTH):
        mod = mods[i]
        kind = i % 3
        g_a = g_norm[i, 0][None, :]
        if kind == 0:
            j = i // 3
            past_k, past_v, past_ki = states[i]
            q, k, v, qi, kw, k4, v4 = _attn_proj(x, mod, g_a, wts['attn_w_main'][j], wts['attn_w_tail'][j],
                                                 tables, tm)
            ki = kw[:, :, :IDX_DIM]
            l_all = past + t
            if past:
                lp = -(-l_all // LANES) * LANES
                kvd = N_KV_HEADS * HEAD_DIM
                k_all = _pad_rows(jnp.concatenate([past_k.reshape(b, past, kvd), k], axis=1), lp)
                v_all = _pad_rows(jnp.concatenate([past_v.reshape(b, past, kvd), v], axis=1), lp)
                ki_all = _pad_rows(jnp.concatenate([past_ki, ki], axis=1), lp)
            else:
                k_all, v_all, ki_all = k, v, ki
            for row0 in range(0, t, tq):
                x = _attn_core(q, qi, kw, k_all, v_all, ki_all, x, mod, wts['attn_w_out'][j], past=past,
                               row0=row0, rows=tq, n_sel=min(TOPK_MAX, l_all // 4), tq=tq)
            new_states.append((k4, v4, ki))
        elif kind == 1:
            x, conv_st = _conv_mixer(x, mod, g_a, wts['sc_w_in'], wts['sc_conv_w'], wts['sc_w_out'],
                                     states[i][0], tm)
            new_states.append((conv_st,))
        else:
            shift_prev, wkv0 = states[i]
            r, k, v, w, a, gg, shift_new = _rwkv_proj(
                x, mod, g_a, shift_prev, wts['rw_mix'], wts['rw_w_rkv'], wts['rw_w1'], wts['rw_w2'], wts['rw_w0'],
                wts['rw_a1'], wts['rw_a2'], wts['rw_a0'], wts['rw_g1'], wts['rw_g2'], tm)
            bh = b * RW_HEADS

            def to_scan(z):
                return z.reshape(b, t, RW_HEADS, RW_HEAD).transpose(1, 3, 0, 2).reshape(t, RW_HEAD, bh)

            def head_tile(p):
                return jnp.tile(p.reshape(RW_HEADS, RW_HEAD).T, (1, b))

            s0 = wkv0.transpose(3, 2, 0, 1).reshape(RW_HEAD, RW_HEAD, bh)
            y, s_fin = _rwkv_scan(to_scan(r), to_scan(k), to_scan(v), to_scan(w), to_scan(a),
                                  head_tile(wts['rw_k_k']), head_tile(wts['rw_k_a']), head_tile(wts['rw_r_k']),
                                  head_tile(wts['rw_ln_g']), head_tile(wts['rw_ln_b']), s0, min(t, SCAN_STEPS))
            y = y.reshape(t, RW_HEAD, b, RW_HEADS).transpose(2, 0, 3, 1).reshape(b, t, d)
            wkv_new = s_fin.reshape(RW_HEAD, RW_HEAD, b, RW_HEADS).transpose(2, 3, 1, 0)
            x = _rwkv_out(y, gg, x, mod, wts['rw_w_o'], tm)
            new_states.append((shift_new, wkv_new))
        x, f_st = _ffn(x, mod, g_norm[i, 1][None, :], wts['ffn_w_up'][i], wts['ffn_conv_w'][i],
                       wts['ffn_conv_b'][i][None, :], wts['ffn_w_down'][i], ffn_left[i],
                       wts['g_final'][None, :], min(t, FFN_ROWS), i == DEPTH - 1)
        new_ffn.append(f_st)
    return x, new_states, jnp.stack(new_ffn)


def _pad_cols(w, cols):
    return jnp.pad(w, ((0, 0), (0, cols - w.shape[1])))


def _pad_rows2(w, rows):
    return jnp.pad(w, ((0, rows - w.shape[0]), (0, 0)))


def kernel(x_prompt, x_sample, c_prompt, c_sample, cache_k_0, cache_v_0, cache_kidx_0, state_conv_1, state_shift_2, state_wkv_2, cache_k_3, cache_v_3, cache_kidx_3, state_ffn_conv, w_mod, b_mod, g_norm, g_final, attn_w_in, attn_w_out, sc_w_in, sc_conv_w, sc_w_out, rw_mix, rw_w_rkv, rw_w_o, rw_w0, rw_w1, rw_w2, rw_a0, rw_a1, rw_a2, rw_g1, rw_g2, rw_k_k, rw_k_a, rw_r_k, rw_ln_g, rw_ln_b, ffn_w_up, ffn_conv_w, ffn_conv_b, ffn_w_down):
    d = D_MODEL
    bp = x_prompt.shape[0]
    bs = x_sample.shape[0]
    dt = x_prompt.dtype

    wts = _prep_weights(g_norm, g_final, attn_w_in, attn_w_out, sc_w_in, sc_conv_w, sc_w_out, rw_mix, rw_w_rkv,
                        rw_w_o, rw_w0, rw_w1, rw_w2, rw_a0, rw_a1, rw_a2, rw_g1, rw_g2, rw_k_k, rw_k_a, rw_r_k,
                        rw_ln_g, rw_ln_b, ffn_w_up, ffn_conv_w, ffn_conv_b, ffn_w_down)

    mods = _mod_all(jnp.concatenate([c_prompt, c_sample], axis=0), w_mod, b_mod)
    mods_p = mods[:, :bp].reshape(DEPTH, bp, 6, d)
    mods_s = mods[:, bp:].reshape(DEPTH, bs, 6, d)

    ffn0 = jnp.zeros((DEPTH, bp, 2, 2 * D_FF), dt)
    y_p, st_p, ffn_p = _trunk(x_prompt, mods_p, _empty_states(bp, dt), ffn0, wts, 0)
    sample_states = [(cache_k_0, cache_v_0, cache_kidx_0), (state_conv_1,), (state_shift_2, state_wkv_2),
                     (cache_k_3, cache_v_3, cache_kidx_3)]
    y_s, st_s, ffn_s = _trunk(x_sample, mods_s, sample_states, state_ffn_conv, wts, cache_k_0.shape[1])
    (k0_p, v0_p, ki0_p), (conv1_p,), (shift2_p, wkv2_p), (k3_p, v3_p, ki3_p) = st_p
    (k0_s, v0_s, ki0_s), (conv1_s,), (shift2_s, wkv2_s), (k3_s, v3_s, ki3_s) = st_s
    return (y_p, y_s,
            k0_p, v0_p, ki0_p, conv1_p, shift2_p, wkv2_p, k3_p, v3_p, ki3_p, ffn_p,
            k0_s, v0_s, ki0_s, conv1_s, shift2_s, wkv2_s, k3_s, v3_s, ki3_s, ffn_s)


def _empty_states(b, dt):
    d = D_MODEL
    kv = (jnp.zeros((b, 0, N_KV_HEADS, HEAD_DIM), dt), jnp.zeros((b, 0, N_KV_HEADS, HEAD_DIM), dt),
          jnp.zeros((b, 0, IDX_DIM), dt))
    return [kv, (jnp.zeros((b, 2, d), dt),),
            (jnp.zeros((b, 1, d), dt), jnp.zeros((b, RW_HEADS, RW_HEAD, RW_HEAD), dt)), kv]


def _prep_weights(g_norm, g_final, attn_w_in, attn_w_out, sc_w_in, sc_conv_w, sc_w_out, rw_mix, rw_w_rkv, rw_w_o,
                  rw_w0, rw_w1, rw_w2, rw_a0, rw_a1, rw_a2, rw_g1, rw_g2, rw_k_k, rw_k_a, rw_r_k, rw_ln_g, rw_ln_b,
                  ffn_w_up, ffn_conv_w, ffn_conv_b, ffn_w_down):
    return {
        'g_norm': g_norm, 'g_final': g_final,
        'attn_w_main': attn_w_in[:, :, :QI_END].astype(BF16),
        'attn_w_tail': jnp.pad(attn_w_in[:, :, QI_END:], ((0, 0), (0, 0), (0, LANES - (WI_END - QI_END)))).astype(BF16),
        'attn_w_out': attn_w_out.astype(BF16),
        'sc_w_in': sc_w_in.astype(BF16), 'sc_conv_w': sc_conv_w, 'sc_w_out': sc_w_out.astype(BF16),
        'rw_mix': rw_mix, 'rw_w_rkv': rw_w_rkv.astype(BF16), 'rw_w_o': rw_w_o.astype(BF16),
        'rw_w0': rw_w0[None, :], 'rw_w1': _pad_cols(rw_w1, LANES).astype(BF16),
        'rw_w2': _pad_rows2(rw_w2, LANES).astype(BF16),
        'rw_a0': rw_a0[None, :], 'rw_a1': _pad_cols(rw_a1, LANES).astype(BF16),
        'rw_a2': _pad_rows2(rw_a2, LANES).astype(BF16),
        'rw_g1': rw_g1.astype(BF16), 'rw_g2': rw_g2.astype(BF16),
        'rw_k_k': rw_k_k, 'rw_k_a': rw_k_a, 'rw_r_k': rw_r_k, 'rw_ln_g': rw_ln_g, 'rw_ln_b': rw_ln_b,
        'ffn_w_up': ffn_w_up.astype(BF16), 'ffn_conv_w': ffn_conv_w, 'ffn_conv_b': ffn_conv_b,
        'ffn_w_down': ffn_w_down.astype(BF16),
    }
```

```python
import functools

import jax
import jax.numpy as jnp
from jax import lax
from jax.experimental import pallas as pl
from jax.experimental.pallas import tpu as pltpu

D_MODEL = 1024
DEPTH = 4
CHUNK = 64
N_HEADS = 8
HEAD_DIM = 128
N_KV_HEADS = 2
GROUP = N_HEADS // N_KV_HEADS
IDX_HEADS = 8
IDX_DIM = 64
TOPK_MAX = 256
ROPE_THETA = 10000.0
RW_HEAD = 64
RW_HEADS = D_MODEL // RW_HEAD
RW_GN_EPS = 64e-5
D_FF = 2 * D_MODEL
NORM_EPS = 1e-6

LANES = 128
SUBLANES = 8
VMEM_LIMIT = 52 * 1024 * 1024
INT_MIN = -2 ** 31
F32_TINY = 1.1754943508222875e-38

F32 = jnp.float32
BF16 = jnp.bfloat16

Q_END = N_HEADS * HEAD_DIM
K_END = Q_END + N_KV_HEADS * HEAD_DIM
V_END = K_END + N_KV_HEADS * HEAD_DIM
QI_END = V_END + IDX_HEADS * IDX_DIM
KI_END = QI_END + IDX_DIM
WI_END = KI_END + IDX_HEADS
Q_SCALE = HEAD_DIM ** -0.5 * 1.4426950408889634


def _cparams(sem):
    return pltpu.CompilerParams(dimension_semantics=sem, vmem_limit_bytes=VMEM_LIMIT)


def _wspec(shape):
    return pl.BlockSpec(shape, lambda bi, i: (0,) * len(shape), pipeline_mode=pl.Buffered(1))


def _dot(a, b):
    return jnp.dot(a, b, preferred_element_type=F32)


def _dot_nt(a, b):
    return lax.dot_general(a, b, (((1,), (1,)), ((), ())), preferred_element_type=F32)


def _dot_nt_split(a, b):
    half = a.shape[0] // 2
    return jnp.concatenate([_dot_nt(a[:half], b), _dot_nt(a[half:], b)], axis=0)


def _dot_ksplit(a, b):
    half = a.shape[1] // 2
    return _dot(a[:, :half], b[:half]) + _dot(a[:, half:], b[half:])


def _norm_mod(x, g, scale, shift):
    ms = jnp.mean(x * x, axis=-1, keepdims=True)
    return (x * lax.rsqrt(ms + NORM_EPS) * g) * (1.0 + scale) + shift


def _shifted(u, hist, n):
    tm, c = u.shape
    tiles = jnp.concatenate([hist, u], axis=0).reshape(tm // SUBLANES + 1, SUBLANES, c)
    row = lax.broadcasted_iota(jnp.int32, (1, SUBLANES, 1), 1)
    outs = []
    for s in range(n, 0, -1):
        rot = pltpu.roll(tiles, s, 1)
        outs.append(jnp.where(row < s, rot[:-1], rot[1:]).reshape(tm, c))
    return outs


def _causal_conv3(u, hist, w0, w1, w2):
    s2, s1 = _shifted(u, hist, 2)
    return w0 * s2 + w1 * s1 + w2 * u


def _mod_kernel(c_ref, w_ref, b_ref, o_ref):
    o_ref[0] = _dot(c_ref[...].astype(BF16), w_ref[0].astype(BF16)) + b_ref[0]


def _mod_all(c_all, w_mod, b_mod):
    nb = c_all.shape[0]
    d = D_MODEL
    return pl.pallas_call(
        _mod_kernel,
        name="adaln_mod",
        out_shape=jax.ShapeDtypeStruct((DEPTH, nb, 6 * d), F32),
        grid=(DEPTH, 6),
        in_specs=[pl.BlockSpec((nb, d), lambda l, n: (0, 0)),
                  pl.BlockSpec((1, d, d), lambda l, n: (l, 0, n)),
                  pl.BlockSpec((1, 1, d), lambda l, n: (l, 0, n))],
        out_specs=pl.BlockSpec((1, nb, d), lambda l, n: (l, 0, n)),
        compiler_params=_cparams(("parallel", "parallel")),
    )(c_all, w_mod, b_mod.reshape(DEPTH, 1, 6 * d))


def _rope128(seg, cos, sin_signed):
    return seg * cos + pltpu.roll(seg, HEAD_DIM // 2, 1) * sin_signed


def _rope64(seg, cos, sin_lo, sin_hi):
    return seg * cos + pltpu.roll(seg, LANES - IDX_DIM // 2, 1) * sin_lo + pltpu.roll(seg, IDX_DIM // 2, 1) * sin_hi


def _attn_proj_kernel(x_ref, mod_ref, g_ref, wm_ref, wt_ref, c128_ref, s128_ref, c64_ref, s64lo_ref,
                      s64hi_ref, ct_ref, stlo_ref, sthi_ref, q_ref, k_ref, v_ref, qi_ref, kw_ref, k4_ref, v4_ref):
    h = _norm_mod(x_ref[0], g_ref[...], mod_ref[0, 1:2, :], mod_ref[0, 0:1, :]).astype(BF16)
    pm = _dot(h, wm_ref[...])
    c128 = c128_ref[...]
    s128 = s128_ref[...]
    for hh in range(N_HEADS):
        lo = hh * HEAD_DIM
        q_ref[0, :, lo:lo + HEAD_DIM] = _rope128(pm[:, lo:lo + HEAD_DIM], c128, s128) * Q_SCALE
    for hh in range(N_KV_HEADS):
        lo = hh * HEAD_DIM
        kh = _rope128(pm[:, Q_END + lo:Q_END + lo + HEAD_DIM], c128, s128)
        k_ref[0, :, lo:lo + HEAD_DIM] = kh
        k4_ref[0, :, hh, :] = kh
        v4_ref[0, :, hh, :] = pm[:, K_END + lo:K_END + lo + HEAD_DIM]
    v_ref[0] = pm[:, K_END:V_END]
    c64 = c64_ref[...]
    s64lo = s64lo_ref[...]
    s64hi = s64hi_ref[...]
    for cc in range(IDX_HEADS * IDX_DIM // LANES):
        lo = cc * LANES
        qi_ref[0, :, lo:lo + LANES] = _rope64(pm[:, V_END + lo:V_END + lo + LANES], c64, s64lo, s64hi)
    pt = _dot(h, wt_ref[...])
    kw_ref[0] = _rope64(pt, ct_ref[...], stlo_ref[...], sthi_ref[...])


def _rope_tables(t_len, past):
    pos = jnp.arange(past, past + t_len, dtype=jnp.int32).astype(F32)[:, None]

    def cs(d):
        half = d // 2
        inv = ROPE_THETA ** (-2.0 * jnp.arange(half, dtype=F32) / d)
        ang = pos * inv[None, :]
        return jnp.cos(ang), jnp.sin(ang)

    c, s = cs(HEAD_DIM)
    c128 = jnp.concatenate([c, c], axis=1)
    s128 = jnp.concatenate([-s, s], axis=1)
    c, s = cs(IDX_DIM)
    z = jnp.zeros_like(s)
    c64 = jnp.concatenate([c, c, c, c], axis=1)
    s64lo = jnp.concatenate([-s, z, -s, z], axis=1)
    s64hi = jnp.concatenate([z, s, z, s], axis=1)
    wscale = jnp.full((t_len, LANES - IDX_DIM), IDX_HEADS ** -0.5, F32)
    ct = jnp.concatenate([c, c, wscale], axis=1)
    stlo = jnp.concatenate([-s, z, z, z], axis=1)
    sthi = jnp.concatenate([z, s, z, z], axis=1)
    return c128, s128, c64, s64lo, s64hi, ct, stlo, sthi


def _attn_proj(x, mod, g, w_main, w_tail, tables, tm):
    b, t, d = x.shape
    tab_spec = pl.BlockSpec((tm, LANES), lambda bi, i: (i, 0))
    const2 = lambda bi, i: (0, 0)
    row3 = lambda bi, i: (bi, i, 0)
    return pl.pallas_call(
        _attn_proj_kernel,
        name="attn_proj",
        out_shape=(jax.ShapeDtypeStruct((b, t, Q_END), F32),
                   jax.ShapeDtypeStruct((b, t, K_END - Q_END), F32),
                   jax.ShapeDtypeStruct((b, t, V_END - K_END), F32),
                   jax.ShapeDtypeStruct((b, t, QI_END - V_END), F32),
                   jax.ShapeDtypeStruct((b, t, LANES), F32),
                   jax.ShapeDtypeStruct((b, t, N_KV_HEADS, HEAD_DIM), F32),
                   jax.ShapeDtypeStruct((b, t, N_KV_HEADS, HEAD_DIM), F32)),
        grid=(b, t // tm),
        in_specs=[pl.BlockSpec((1, tm, d), row3),
                  pl.BlockSpec((1, 6, d), lambda bi, i: (bi, 0, 0)),
                  pl.BlockSpec((1, d), const2),
                  _wspec((d, QI_END)),
                  _wspec((d, LANES))] + [tab_spec] * 8,
        out_specs=(pl.BlockSpec((1, tm, Q_END), row3),
                   pl.BlockSpec((1, tm, K_END - Q_END), row3),
                   pl.BlockSpec((1, tm, V_END - K_END), row3),
                   pl.BlockSpec((1, tm, QI_END - V_END), row3),
                   pl.BlockSpec((1, tm, LANES), row3),
                   pl.BlockSpec((1, tm, N_KV_HEADS, HEAD_DIM), lambda bi, i: (bi, i, 0, 0)),
                   pl.BlockSpec((1, tm, N_KV_HEADS, HEAD_DIM), lambda bi, i: (bi, i, 0, 0))),
        compiler_params=_cparams(("parallel", "parallel")),
    )(x, mod, g, w_main, w_tail, *tables)


HEADS_PER_DOT = 4


def _col_reduce(x, op):
    rows, cols = x.shape
    slab = 8 * SUBLANES
    if rows % slab == 0 and rows > slab:
        x = op(x.reshape(rows // slab, slab, cols), axis=0)
    return op(x, axis=0, keepdims=True)


def _attn_core_kernel(q_ref, qi_ref, kw_ref, k_ref, v_ref, ki_ref, x_ref, mod_ref, wo_ref, o_ref,
                      *, pos0, l_true, n_sel):
    tq = q_ref.shape[1]
    lp = k_ref.shape[1]
    i = pl.program_id(1)
    qpos = pos0 + i * tq + lax.broadcasted_iota(jnp.int32, (1, tq), 1)
    kidx = lax.broadcasted_iota(jnp.int32, (lp, tq), 0)
    adm = (lax.shift_right_logical(kidx, 6) <= lax.shift_right_logical(qpos, 6)) & (kidx < l_true)
    sel = adm if l_true <= n_sel else _topk_mask(qi_ref, kw_ref, ki_ref, adm, kidx, n_sel)

    outs = []
    for kv in range(N_KV_HEADS):
        kk = k_ref[0, :, kv * HEAD_DIM:(kv + 1) * HEAD_DIM].astype(BF16)
        vt = v_ref[0, :, kv * HEAD_DIM:(kv + 1) * HEAD_DIM].T.astype(BF16)
        for g0 in range(0, GROUP, HEADS_PER_DOT):
            lo = (kv * GROUP + g0) * HEAD_DIM
            qcat = jnp.concatenate([q_ref[0, :, lo + n * HEAD_DIM:lo + (n + 1) * HEAD_DIM]
                                    for n in range(HEADS_PER_DOT)], axis=0).astype(BF16)
            sn = _dot_nt(kk, qcat)
            ps = []
            ls = []
            for n in range(HEADS_PER_DOT):
                s = jnp.where(sel, sn[:, n * tq:(n + 1) * tq], -jnp.inf)
                m = _col_reduce(s, jnp.max)
                p = jnp.exp2(s - m)
                ls.append(_col_reduce(p, jnp.sum))
                ps.append(p.astype(BF16))
            on = _dot(vt, jnp.concatenate(ps, axis=1))
            for n in range(HEADS_PER_DOT):
                outs.append(on[:, n * tq:(n + 1) * tq] / ls[n])
    o = jnp.concatenate(outs, axis=0).T.astype(BF16)
    o_ref[0] = x_ref[0] + mod_ref[0, 2:3, :] * _dot(o, wo_ref[...])


def _topk_mask(qi_ref, kw_ref, ki_ref, adm, kidx, n_sel):
    lp, tq = kidx.shape
    kib = ki_ref[0].astype(BF16)
    wit = kw_ref[0].T
    score = jnp.zeros((lp, tq), F32)
    for h0 in range(0, IDX_HEADS, HEADS_PER_DOT):
        qcat = jnp.concatenate([qi_ref[0, :, (h0 + n) * IDX_DIM:(h0 + n + 1) * IDX_DIM]
                                for n in range(HEADS_PER_DOT)], axis=0).astype(BF16)
        lg = _dot_nt(kib, qcat)
        for n in range(HEADS_PER_DOT):
            w_row = wit[IDX_DIM + h0 + n:IDX_DIM + h0 + n + 1, :]
            score = score + w_row * jnp.maximum(lg[:, n * tq:(n + 1) * tq], 0.0)

    score = jnp.where(jnp.abs(score) < F32_TINY, 0.0, score)
    bits = lax.bitcast_convert_type(score, jnp.int32)
    key = bits ^ ((bits >> 31) & jnp.int32(0x7FFFFFFF))
    key = jnp.where(adm, key, jnp.int32(INT_MIN))
    kf = jnp.float32(n_sel)

    def thr_body(it, t):
        cand = t + lax.shift_left(jnp.int32(1), 31 - it)
        cnt = _col_reduce(jnp.where(key >= cand, 1.0, 0.0), jnp.sum)
        return jnp.where(cnt >= kf, cand, t)

    thr = lax.fori_loop(0, 32, thr_body, jnp.full((1, tq), INT_MIN, jnp.int32))
    gt = key > thr
    eq = (key == thr) & adm
    need = kf - _col_reduce(jnp.where(gt, 1.0, 0.0), jnp.sum)
    eqf = jnp.where(eq, 1.0, 0.0)
    excess = _col_reduce(eqf, jnp.sum) - need

    def tie_search():
        nbits = lp.bit_length()

        def body(it, m):
            cand = m + lax.shift_left(jnp.int32(1), nbits - 1 - it)
            f = _col_reduce(jnp.where(kidx < cand, eqf, 0.0), jnp.sum)
            return jnp.where(f < need, cand, m)

        return lax.fori_loop(0, nbits, body, jnp.zeros((1, tq), jnp.int32)) + 1

    jstar = lax.cond(jnp.max(excess) > 0.0, tie_search, lambda: jnp.full((1, tq), lp, jnp.int32))
    return gt | (eq & (kidx < jstar))


def _attn_core(q, qi, kw, k_all, v_all, ki_all, x, mod, w_out, *, past, row0, rows, n_sel, tq):
    b, t, d = x.shape
    l_true = past + row0 + rows
    lp = -(-l_true // LANES) * LANES
    blk0 = row0 // tq
    row3 = lambda bi, i: (bi, blk0 + i, 0)
    per_b = lambda bi, i: (bi, 0, 0)
    kern = functools.partial(_attn_core_kernel, pos0=past + row0, l_true=l_true, n_sel=n_sel)
    return pl.pallas_call(
        kern,
        name=f"attn_core_l{lp}",
        out_shape=jax.ShapeDtypeStruct((b, t, d), F32),
        input_output_aliases={6: 0},
        grid=(b, rows // tq),
        in_specs=[pl.BlockSpec((1, tq, Q_END), row3),
                  pl.BlockSpec((1, tq, QI_END - V_END), row3),
                  pl.BlockSpec((1, tq, LANES), row3),
                  pl.BlockSpec((1, lp, K_END - Q_END), per_b),
                  pl.BlockSpec((1, lp, V_END - K_END), per_b),
                  pl.BlockSpec((1, lp, IDX_DIM), per_b),
                  pl.BlockSpec((1, tq, d), row3),
                  pl.BlockSpec((1, 6, d), per_b),
                  _wspec((N_HEADS * HEAD_DIM, d))],
        out_specs=pl.BlockSpec((1, tq, d), row3),
        compiler_params=_cparams(("parallel", "parallel")),
    )(q, qi, kw, k_all, v_all, ki_all, x, mod, w_out)


def _conv_mixer_kernel(x_ref, mod_ref, g_ref, win_ref, cw_ref, wout_ref, left_ref, o_ref, st_ref, carry_ref):
    d = D_MODEL
    tm = x_ref.shape[1]

    @pl.when(pl.program_id(1) == 0)
    def _():
        carry_ref[...] = jnp.zeros_like(carry_ref)
        carry_ref[6:8, :] = left_ref[0]

    x = x_ref[0]
    h = _norm_mod(x, g_ref[...], mod_ref[0, 1:2, :], mod_ref[0, 0:1, :]).astype(BF16)
    gb = _dot(h, win_ref[:, 0:d])
    gc = _dot(h, win_ref[:, d:2 * d])
    u = gc * _dot(h, win_ref[:, 2 * d:3 * d])
    y = _causal_conv3(u, carry_ref[...], cw_ref[0:1, :], cw_ref[1:2, :], cw_ref[2:3, :])
    carry_ref[...] = u[tm - SUBLANES:tm, :]
    st_ref[0] = u[tm - 2:tm, :]
    o_ref[0] = x + mod_ref[0, 2:3, :] * _dot((gb * y).astype(BF16), wout_ref[...])


def _conv_mixer(x, mod, g, w_in, conv_w, w_out, left, tm):
    b, t, d = x.shape
    row3 = lambda bi, i: (bi, i, 0)
    per_b = lambda bi, i: (bi, 0, 0)
    const2 = lambda bi, i: (0, 0)
    return pl.pallas_call(
        _conv_mixer_kernel,
        name="conv_mixer",
        out_shape=(jax.ShapeDtypeStruct((b, t, d), F32), jax.ShapeDtypeStruct((b, 2, d), F32)),
        grid=(b, t // tm),
        in_specs=[pl.BlockSpec((1, tm, d), row3),
                  pl.BlockSpec((1, 6, d), per_b),
                  pl.BlockSpec((1, d), const2),
                  _wspec((d, 3 * d)),
                  pl.BlockSpec((3, d), const2),
                  _wspec((d, d)),
                  pl.BlockSpec((1, 2, d), per_b)],
        out_specs=(pl.BlockSpec((1, tm, d), row3), pl.BlockSpec((1, 2, d), per_b)),
        scratch_shapes=[pltpu.VMEM((SUBLANES, d), F32)],
        compiler_params=_cparams(("parallel", "arbitrary")),
    )(x, mod, g, w_in, conv_w, w_out, left)


FFN_COLS = 512
FFN_ROWS = 1024


def _ffn_kernel(x_ref, mod_ref, g_ref, wup_ref, cw_ref, cb_ref, wdn_ref, left_ref, gfin_ref, o_ref, st_ref,
                carry_ref, *, final_norm):
    tm = x_ref.shape[1]

    @pl.when(pl.program_id(1) == 0)
    def _():
        carry_ref[...] = jnp.zeros_like(carry_ref)
        carry_ref[6:8, :] = left_ref[0]

    x = x_ref[0]
    h = _norm_mod(x, g_ref[...], mod_ref[0, 4:5, :], mod_ref[0, 3:4, :]).astype(BF16)

    def up_cols(c):
        return _dot(h, wup_ref[:, c:c + FFN_COLS]), _dot(h, wup_ref[:, D_FF + c:D_FF + c + FFN_COLS])

    def conv_cols(u, lo):
        z = _causal_conv3(u, carry_ref[:, lo:lo + FFN_COLS], cw_ref[0:1, lo:lo + FFN_COLS],
                          cw_ref[1:2, lo:lo + FFN_COLS], cw_ref[2:3, lo:lo + FFN_COLS]) + cb_ref[:, lo:lo + FFN_COLS]
        carry_ref[:, lo:lo + FFN_COLS] = u[tm - SUBLANES:tm, :]
        st_ref[0, :, lo:lo + FFN_COLS] = u[tm - 2:tm, :]
        return z

    acc = jnp.zeros((tm, D_MODEL), F32)
    u_next = up_cols(0)
    for c in range(0, D_FF, FFN_COLS):
        u_gate, u_val = u_next
        if c + FFN_COLS < D_FF:
            u_next = up_cols(c + FFN_COLS)
        gate = conv_cols(u_gate, c)
        val = conv_cols(u_val, D_FF + c)
        act = (gate * jax.nn.sigmoid(gate)) * val
        acc = acc + _dot(act.astype(BF16), wdn_ref[c:c + FFN_COLS, :])
    y = x + mod_ref[0, 5:6, :] * acc
    if final_norm:
        ms = jnp.mean(y * y, axis=-1, keepdims=True)
        y = y * lax.rsqrt(ms + NORM_EPS) * gfin_ref[...]
    o_ref[0] = y


def _ffn(x, mod, g, w_up, conv_w, conv_b, w_down, left, g_final, tm, final_norm):
    b, t, d = x.shape
    row3 = lambda bi, i: (bi, i, 0)
    per_b = lambda bi, i: (bi, 0, 0)
    const2 = lambda bi, i: (0, 0)
    kern = functools.partial(_ffn_kernel, final_norm=final_norm)
    return pl.pallas_call(
        kern,
        name="conv_ffn",
        out_shape=(jax.ShapeDtypeStruct((b, t, d), F32), jax.ShapeDtypeStruct((b, 2, 2 * D_FF), F32)),
        grid=(b, t // tm),
        in_specs=[pl.BlockSpec((1, tm, d), row3),
                  pl.BlockSpec((1, 6, d), per_b),
                  pl.BlockSpec((1, d), const2),
                  _wspec((d, 2 * D_FF)),
                  pl.BlockSpec((3, 2 * D_FF), const2),
                  pl.BlockSpec((1, 2 * D_FF), const2),
                  _wspec((D_FF, d)),
                  pl.BlockSpec((1, 2, 2 * D_FF), per_b),
                  pl.BlockSpec((1, d), const2)],
        out_specs=(pl.BlockSpec((1, tm, d), row3), pl.BlockSpec((1, 2, 2 * D_FF), per_b)),
        scratch_shapes=[pltpu.VMEM((SUBLANES, 2 * D_FF), F32)],
        compiler_params=_cparams(("parallel", "arbitrary")),
    )(x, mod, g, w_up, conv_w, conv_b, w_down, left, g_final)


def _softplus(z):
    return jnp.maximum(z, 0.0) + jnp.log1p(jnp.exp(-jnp.abs(z)))


def _rwkv_proj_kernel(x_ref, mod_ref, g_ref, sp_ref, mix_ref, wrkv_ref, w1_ref, w2_ref, w0_ref, a1_ref, a2_ref,
                      a0_ref, g1_ref, g2_ref, r_ref, k_ref, v_ref, w_ref, a_ref, gg_ref, sh_ref, carry_ref):
    tm = x_ref.shape[1]

    @pl.when(pl.program_id(1) == 0)
    def _():
        carry_ref[...] = jnp.zeros_like(carry_ref)
        carry_ref[7:8, :] = sp_ref[0]

    h = _norm_mod(x_ref[0], g_ref[...], mod_ref[0, 1:2, :], mod_ref[0, 0:1, :])
    (hs,) = _shifted(h, carry_ref[...], 1)
    carry_ref[...] = h[tm - SUBLANES:tm, :]
    sh_ref[0] = h[tm - 1:tm, :]
    xx = hs - h

    def mixed(j):
        return (h + xx * mix_ref[j:j + 1, :]).astype(BF16)

    r_ref[0] = _dot(mixed(0), wrkv_ref[0])
    k_ref[0] = _dot(mixed(2), wrkv_ref[1])
    v_ref[0] = _dot(mixed(3), wrkv_ref[2])
    lw = _dot(jnp.tanh(_dot(mixed(1), w1_ref[...])).astype(BF16), w2_ref[...])
    w_log = -_softplus(-(w0_ref[...] + lw)) - 0.5
    w_ref[0] = jnp.exp(-jnp.exp(w_log))
    a_ref[0] = jax.nn.sigmoid(a0_ref[...] + _dot(_dot(mixed(4), a1_ref[...]).astype(BF16), a2_ref[...]))
    gg_ref[0] = _dot(jax.nn.sigmoid(_dot(mixed(5), g1_ref[...])).astype(BF16), g2_ref[...])


def _rwkv_proj(x, mod, g, shift_prev, mix, w_rkv, w1, w2, w0, a1, a2, a0, g1, g2, tm):
    b, t, d = x.shape
    row3 = lambda bi, i: (bi, i, 0)
    per_b = lambda bi, i: (bi, 0, 0)
    const2 = lambda bi, i: (0, 0)
    big = jax.ShapeDtypeStruct((b, t, d), F32)
    lora = w1.shape[1]
    gl = g1.shape[1]
    return pl.pallas_call(
        _rwkv_proj_kernel,
        name="rwkv_proj",
        out_shape=(big,) * 6 + (jax.ShapeDtypeStruct((b, 1, d), F32),),
        grid=(b, t // tm),
        in_specs=[pl.BlockSpec((1, tm, d), row3),
                  pl.BlockSpec((1, 6, d), per_b),
                  pl.BlockSpec((1, d), const2),
                  pl.BlockSpec((1, 1, d), per_b),
                  pl.BlockSpec((6, d), const2),
                  _wspec((3, d, d)),
                  pl.BlockSpec((d, lora), const2),
                  pl.BlockSpec((lora, d), const2),
                  pl.BlockSpec((1, d), const2),
                  pl.BlockSpec((d, lora), const2),
                  pl.BlockSpec((lora, d), const2),
                  pl.BlockSpec((1, d), const2),
                  pl.BlockSpec((d, gl), const2),
                  pl.BlockSpec((gl, d), const2)],
        out_specs=(pl.BlockSpec((1, tm, d), row3),) * 6 + (pl.BlockSpec((1, 1, d), per_b),),
        scratch_shapes=[pltpu.VMEM((SUBLANES, d), F32)],
        compiler_params=_cparams(("parallel", "arbitrary")),
    )(x, mod, g, shift_prev, mix, w_rkv, w1, w2, w0, a1, a2, a0, g1, g2)


SCAN_STEPS = 64
def _rwkv_scan_kernel(r_ref, k_ref, v_ref, w_ref, a_ref, kk_ref, ka_ref, rk_ref, lg_ref, lb_ref, s0_ref,
                      y_ref, sfin_ref, s_ref, bc_ref):
    tc = r_ref.shape[0]
    n = RW_HEAD

    @pl.when(pl.program_id(1) == 0)
    def _():
        s_ref[...] = s0_ref[...]

    def unit_kk(t):
        kkr = k_ref[t] * kk_ref[...]
        nrm = jnp.sqrt(jnp.sum(kkr * kkr, axis=0, keepdims=True))
        return kkr / jnp.maximum(nrm, 1e-12)

    def step(t, carry, has_next):
        cum_prev, sa, kk = carry
        kt = k_ref[t]
        at = a_ref[t]
        rt = r_ref[t]
        vt = v_ref[t]
        cum = cum_prev * w_ref[t]
        inv = 1.0 / cum
        kmod = kt * (1.0 + (at - 1.0) * ka_ref[...])
        bc_ref[0] = (kk * at) * inv
        bc_ref[1] = kmod * inv
        bc_ref[2] = rt * cum
        if has_next:
            kk_next = unit_kk(t + 1)
            bc_ref[3] = -kk_next * cum
        y = jnp.zeros((n, LANES), F32)
        sa_next = jnp.zeros((n, LANES), F32)
        for j in range(n):
            sj = s_ref[j] + sa * bc_ref[0, j:j + 1, :] + vt * bc_ref[1, j:j + 1, :]
            s_ref[j] = sj
            y = y + sj * bc_ref[2, j:j + 1, :]
            if has_next:
                sa_next = sa_next + sj * bc_ref[3, j:j + 1, :]
        mu = jnp.mean(y, axis=0, keepdims=True)
        yc = y - mu
        var = jnp.mean(yc * yc, axis=0, keepdims=True)
        yn = (yc * lax.rsqrt(var + RW_GN_EPS)) * lg_ref[...] + lb_ref[...]
        bonus = jnp.sum(rt * kmod * rk_ref[...], axis=0, keepdims=True) * vt
        y_ref[t] = yn + bonus
        if has_next:
            return cum, sa_next, kk_next
        return cum

    kk0 = unit_kk(0)
    bc_ref[3] = -kk0
    sa0 = jnp.zeros((n, LANES), F32)
    for j in range(n):
        sa0 = sa0 + s_ref[j] * bc_ref[3, j:j + 1, :]
    carry = (jnp.ones((n, LANES), F32), sa0, kk0)
    carry = lax.fori_loop(0, tc - 1, lambda t, c: step(t, c, True), carry)
    bc_ref[0] = step(tc - 1, carry, False)
    for j in range(n):
        s_ref[j] = s_ref[j] * bc_ref[0, j:j + 1, :]
    sfin_ref[...] = s_ref[...]


def _rwkv_scan(r, k, v, w, a, kk_t, ka_t, rk_t, lg_t, lb_t, s0, tc):
    t, n, bh = r.shape
    seq = pl.BlockSpec((tc, n, LANES), lambda lb, ti: (ti, 0, lb))
    par = pl.BlockSpec((n, LANES), lambda lb, ti: (0, lb))
    st = pl.BlockSpec((n, n, LANES), lambda lb, ti: (0, 0, lb))
    return pl.pallas_call(
        _rwkv_scan_kernel,
        name="rwkv_scan",
        out_shape=(jax.ShapeDtypeStruct((t, n, bh), F32), jax.ShapeDtypeStruct((n, n, bh), F32)),
        grid=(bh // LANES, t // tc),
        in_specs=[seq] * 5 + [par] * 5 + [st],
        out_specs=(seq, st),
        scratch_shapes=[pltpu.VMEM((n, n, LANES), F32), pltpu.VMEM((5, n, LANES), F32)],
        compiler_params=_cparams(("parallel", "arbitrary")),
    )(r, k, v, w, a, kk_t, ka_t, rk_t, lg_t, lb_t, s0)


def _rwkv_out_kernel(y_ref, gg_ref, x_ref, mod_ref, wo_ref, o_ref):
    o_ref[0] = x_ref[0] + mod_ref[0, 2:3, :] * _dot((y_ref[0] * gg_ref[0]).astype(BF16), wo_ref[...])


def _rwkv_out(y, gg, x, mod, w_o, tm):
    b, t, d = x.shape
    row3 = lambda bi, i: (bi, i, 0)
    return pl.pallas_call(
        _rwkv_out_kernel,
        name="rwkv_out",
        out_shape=jax.ShapeDtypeStruct((b, t, d), F32),
        grid=(b, t // tm),
        in_specs=[pl.BlockSpec((1, tm, d), row3), pl.BlockSpec((1, tm, d), row3), pl.BlockSpec((1, tm, d), row3),
                  pl.BlockSpec((1, 6, d), lambda bi, i: (bi, 0, 0)),
                  _wspec((d, d))],
        out_specs=pl.BlockSpec((1, tm, d), row3),
        compiler_params=_cparams(("parallel", "parallel")),
    )(y, gg, x, mod, w_o)


def _pad_rows(a, rows):
    return jnp.pad(a, ((0, 0), (0, rows - a.shape[1]), (0, 0)))


def _trunk(x, mods, states, ffn_left, wts, past):
    b, t, d = x.shape
    tm = min(t, 512)
    tq = min(t, 256)
    g_norm = wts['g_norm']
    new_states = []
    new_ffn = []
    tables = _rope_tables(t, past)
    for i in range(DEPTH):
        mod = mods[i]
        kind = i % 3
        g_a = g_norm[i, 0][None, :]
        if kind == 0:
            j = i // 3
            past_k, past_v, past_ki = states[i]
            q, k, v, qi, kw, k4, v4 = _attn_proj(x, mod, g_a, wts['attn_w_main'][j], wts['attn_w_tail'][j],
                                                 tables, tm)
            ki = kw[:, :, :IDX_DIM]
            l_all = past + t
            if past:
                lp = -(-l_all // LANES) * LANES
                kvd = N_KV_HEADS * HEAD_DIM
                k_all = _pad_rows(jnp.concatenate([past_k.reshape(b, past, kvd), k], axis=1), lp)
                v_all = _pad_rows(jnp.concatenate([past_v.reshape(b, past, kvd), v], axis=1), lp)
                ki_all = _pad_rows(jnp.concatenate([past_ki, ki], axis=1), lp)
            else:
                k_all, v_all, ki_all = k, v, ki
            for row0 in range(0, t, tq):
                x = _attn_core(q, qi, kw, k_all, v_all, ki_all, x, mod, wts['attn_w_out'][j], past=past,
                               row0=row0, rows=tq, n_sel=min(TOPK_MAX, l_all // 4), tq=tq)
            new_states.append((k4, v4, ki))
        elif kind == 1:
            x, conv_st = _conv_mixer(x, mod, g_a, wts['sc_w_in'], wts['sc_conv_w'], wts['sc_w_out'],
                                     states[i][0], tm)
            new_states.append((conv_st,))
        else:
            shift_prev, wkv0 = states[i]
            r, k, v, w, a, gg, shift_new = _rwkv_proj(
                x, mod, g_a, shift_prev, wts['rw_mix'], wts['rw_w_rkv'], wts['rw_w1'], wts['rw_w2'], wts['rw_w0'],
                wts['rw_a1'], wts['rw_a2'], wts['rw_a0'], wts['rw_g1'], wts['rw_g2'], tm)
            bh = b * RW_HEADS

            def to_scan(z):
                return z.reshape(b, t, RW_HEADS, RW_HEAD).transpose(1, 3, 0, 2).reshape(t, RW_HEAD, bh)

            def head_tile(p):
                return jnp.tile(p.reshape(RW_HEADS, RW_HEAD).T, (1, b))

            s0 = wkv0.transpose(3, 2, 0, 1).reshape(RW_HEAD, RW_HEAD, bh)
            y, s_fin = _rwkv_scan(to_scan(r), to_scan(k), to_scan(v), to_scan(w), to_scan(a),
                                  head_tile(wts['rw_k_k']), head_tile(wts['rw_k_a']), head_tile(wts['rw_r_k']),
                                  head_tile(wts['rw_ln_g']), head_tile(wts['rw_ln_b']), s0, min(t, SCAN_STEPS))
            y = y.reshape(t, RW_HEAD, b, RW_HEADS).transpose(2, 0, 3, 1).reshape(b, t, d)
            wkv_new = s_fin.reshape(RW_HEAD, RW_HEAD, b, RW_HEADS).transpose(2, 3, 1, 0)
            x = _rwkv_out(y, gg, x, mod, wts['rw_w_o'], tm)
            new_states.append((shift_new, wkv_new))
        x, f_st = _ffn(x, mod, g_norm[i, 1][None, :], wts['ffn_w_up'][i], wts['ffn_conv_w'][i],
                       wts['ffn_conv_b'][i][None, :], wts['ffn_w_down'][i], ffn_left[i],
                       wts['g_final'][None, :], min(t, FFN_ROWS), i == DEPTH - 1)
        new_ffn.append(f_st)
    return x, new_states, jnp.stack(new_ffn)


def _pad_cols(w, cols):
    return jnp.pad(w, ((0, 0), (0, cols - w.shape[1])))


def _pad_rows2(w, rows):
    return jnp.pad(w, ((0, rows - w.shape[0]), (0, 0)))


def kernel(x_prompt, x_sample, c_prompt, c_sample, cache_k_0, cache_v_0, cache_kidx_0, state_conv_1, state_shift_2, state_wkv_2, cache_k_3, cache_v_3, cache_kidx_3, state_ffn_conv, w_mod, b_mod, g_norm, g_final, attn_w_in, attn_w_out, sc_w_in, sc_conv_w, sc_w_out, rw_mix, rw_w_rkv, rw_w_o, rw_w0, rw_w1, rw_w2, rw_a0, rw_a1, rw_a2, rw_g1, rw_g2, rw_k_k, rw_k_a, rw_r_k, rw_ln_g, rw_ln_b, ffn_w_up, ffn_conv_w, ffn_conv_b, ffn_w_down):
    d = D_MODEL
    bp = x_prompt.shape[0]
    bs = x_sample.shape[0]
    dt = x_prompt.dtype

    wts = _prep_weights(g_norm, g_final, attn_w_in, attn_w_out, sc_w_in, sc_conv_w, sc_w_out, rw_mix, rw_w_rkv,
                        rw_w_o, rw_w0, rw_w1, rw_w2, rw_a0, rw_a1, rw_a2, rw_g1, rw_g2, rw_k_k, rw_k_a, rw_r_k,
                        rw_ln_g, rw_ln_b, ffn_w_up, ffn_conv_w, ffn_conv_b, ffn_w_down)

    mods = _mod_all(jnp.concatenate([c_prompt, c_sample], axis=0), w_mod, b_mod)
    mods_p = mods[:, :bp].reshape(DEPTH, bp, 6, d)
    mods_s = mods[:, bp:].reshape(DEPTH, bs, 6, d)

    ffn0 = jnp.zeros((DEPTH, bp, 2, 2 * D_FF), dt)
    y_p, st_p, ffn_p = _trunk(x_prompt, mods_p, _empty_states(bp, dt), ffn0, wts, 0)
    sample_states = [(cache_k_0, cache_v_0, cache_kidx_0), (state_conv_1,), (state_shift_2, state_wkv_2),
                     (cache_k_3, cache_v_3, cache_kidx_3)]
    y_s, st_s, ffn_s = _trunk(x_sample, mods_s, sample_states, state_ffn_conv, wts, cache_k_0.shape[1])
    (k0_p, v0_p, ki0_p), (conv1_p,), (shift2_p, wkv2_p), (k3_p, v3_p, ki3_p) = st_p
    (k0_s, v0_s, ki0_s), (conv1_s,), (shift2_s, wkv2_s), (k3_s, v3_s, ki3_s) = st_s
    return (y_p, y_s,
            k0_p, v0_p, ki0_p, conv1_p, shift2_p, wkv2_p, k3_p, v3_p, ki3_p, ffn_p,
            k0_s, v0_s, ki0_s, conv1_s, shift2_s, wkv2_s, k3_s, v3_s, ki3_s, ffn_s)


def _empty_states(b, dt):
    d = D_MODEL
    kv = (jnp.zeros((b, 0, N_KV_HEADS, HEAD_DIM), dt), jnp.zeros((b, 0, N_KV_HEADS, HEAD_DIM), dt),
          jnp.zeros((b, 0, IDX_DIM), dt))
    return [kv, (jnp.zeros((b, 2, d), dt),),
            (jnp.zeros((b, 1, d), dt), jnp.zeros((b, RW_HEADS, RW_HEAD, RW_HEAD), dt)), kv]


def _prep_weights(g_norm, g_final, attn_w_in, attn_w_out, sc_w_in, sc_conv_w, sc_w_out, rw_mix, rw_w_rkv, rw_w_o,
                  rw_w0, rw_w1, rw_w2, rw_a0, rw_a1, rw_a2, rw_g1, rw_g2, rw_k_k, rw_k_a, rw_r_k, rw_ln_g, rw_ln_b,
                  ffn_w_up, ffn_conv_w, ffn_conv_b, ffn_w_down):
    return {
        'g_norm': g_norm, 'g_final': g_final,
        'attn_w_main': attn_w_in[:, :, :QI_END].astype(BF16),
        'attn_w_tail': jnp.pad(attn_w_in[:, :, QI_END:], ((0, 0), (0, 0), (0, LANES - (WI_END - QI_END)))).astype(BF16),
        'attn_w_out': attn_w_out.astype(BF16),
        'sc_w_in': sc_w_in.astype(BF16), 'sc_conv_w': sc_conv_w, 'sc_w_out': sc_w_out.astype(BF16),
        'rw_mix': rw_mix, 'rw_w_rkv': rw_w_rkv.astype(BF16), 'rw_w_o': rw_w_o.astype(BF16),
        'rw_w0': rw_w0[None, :], 'rw_w1': _pad_cols(rw_w1, LANES).astype(BF16),
        'rw_w2': _pad_rows2(rw_w2, LANES).astype(BF16),
        'rw_a0': rw_a0[None, :], 'rw_a1': _pad_cols(rw_a1, LANES).astype(BF16),
        'rw_a2': _pad_rows2(rw_a2, LANES).astype(BF16),
        'rw_g1': rw_g1.astype(BF16), 'rw_g2': rw_g2.astype(BF16),
        'rw_k_k': rw_k_k, 'rw_k_a': rw_k_a, 'rw_r_k': rw_r_k, 'rw_ln_g': rw_ln_g, 'rw_ln_b': rw_ln_b,
        'ffn_w_up': ffn_w_up.astype(BF16), 'ffn_conv_w': ffn_conv_w, 'ffn_conv_b': ffn_conv_b,
        'ffn_w_down': ffn_w_down.astype(BF16),
    }
```

```python
import functools

import jax
import jax.numpy as jnp
from jax import lax
from jax.experimental import pallas as pl
from jax.experimental.pallas import tpu as pltpu

D_MODEL = 1024
DEPTH = 4
CHUNK = 64
N_HEADS = 8
HEAD_DIM = 128
N_KV_HEADS = 2
GROUP = N_HEADS // N_KV_HEADS
IDX_HEADS = 8
IDX_DIM = 64
TOPK_MAX = 256
ROPE_THETA = 10000.0
RW_HEAD = 64
RW_HEADS = D_MODEL // RW_HEAD
RW_GN_EPS = 64e-5
D_FF = 2 * D_MODEL
NORM_EPS = 1e-6

LANES = 128
SUBLANES = 8
VMEM_LIMIT = 52 * 1024 * 1024
INT_MIN = -2 ** 31
F32_TINY = 1.1754943508222875e-38

F32 = jnp.float32
BF16 = jnp.bfloat16

Q_END = N_HEADS * HEAD_DIM
K_END = Q_END + N_KV_HEADS * HEAD_DIM
V_END = K_END + N_KV_HEADS * HEAD_DIM
QI_END = V_END + IDX_HEADS * IDX_DIM
KI_END = QI_END + IDX_DIM
WI_END = KI_END + IDX_HEADS
Q_SCALE = HEAD_DIM ** -0.5 * 1.4426950408889634


def _cparams(sem):
    return pltpu.CompilerParams(dimension_semantics=sem, vmem_limit_bytes=VMEM_LIMIT)


def _wspec(shape):
    return pl.BlockSpec(shape, lambda bi, i: (0,) * len(shape), pipeline_mode=pl.Buffered(1))


def _dot(a, b):
    return jnp.dot(a, b, preferred_element_type=F32)


def _dot_nt(a, b):
    return lax.dot_general(a, b, (((1,), (1,)), ((), ())), preferred_element_type=F32)


def _dot_nt_split(a, b):
    half = a.shape[0] // 2
    return jnp.concatenate([_dot_nt(a[:half], b), _dot_nt(a[half:], b)], axis=0)


def _dot_ksplit(a, b):
    half = a.shape[1] // 2
    return _dot(a[:, :half], b[:half]) + _dot(a[:, half:], b[half:])


def _norm_mod(x, g, scale, shift):
    ms = jnp.mean(x * x, axis=-1, keepdims=True)
    return (x * lax.rsqrt(ms + NORM_EPS) * g) * (1.0 + scale) + shift


def _shifted(u, hist, n):
    tm, c = u.shape
    tiles = jnp.concatenate([hist, u], axis=0).reshape(tm // SUBLANES + 1, SUBLANES, c)
    row = lax.broadcasted_iota(jnp.int32, (1, SUBLANES, 1), 1)
    outs = []
    for s in range(n, 0, -1):
        rot = pltpu.roll(tiles, s, 1)
        outs.append(jnp.where(row < s, rot[:-1], rot[1:]).reshape(tm, c))
    return outs


def _causal_conv3(u, hist, w0, w1, w2):
    s2, s1 = _shifted(u, hist, 2)
    return w0 * s2 + w1 * s1 + w2 * u


def _mod_kernel(c_ref, w_ref, b_ref, o_ref):
    o_ref[0] = _dot(c_ref[...].astype(BF16), w_ref[0].astype(BF16)) + b_ref[0]


def _mod_all(c_all, w_mod, b_mod):
    nb = c_all.shape[0]
    d = D_MODEL
    return pl.pallas_call(
        _mod_kernel,
        name="adaln_mod",
        out_shape=jax.ShapeDtypeStruct((DEPTH, nb, 6 * d), F32),
        grid=(DEPTH, 6),
        in_specs=[pl.BlockSpec((nb, d), lambda l, n: (0, 0)),
                  pl.BlockSpec((1, d, d), lambda l, n: (l, 0, n)),
                  pl.BlockSpec((1, 1, d), lambda l, n: (l, 0, n))],
        out_specs=pl.BlockSpec((1, nb, d), lambda l, n: (l, 0, n)),
        compiler_params=_cparams(("parallel", "parallel")),
    )(c_all, w_mod, b_mod.reshape(DEPTH, 1, 6 * d))


def _rope128(seg, cos, sin_signed):
    return seg * cos + pltpu.roll(seg, HEAD_DIM // 2, 1) * sin_signed


def _rope64(seg, cos, sin_lo, sin_hi):
    return seg * cos + pltpu.roll(seg, LANES - IDX_DIM // 2, 1) * sin_lo + pltpu.roll(seg, IDX_DIM // 2, 1) * sin_hi


def _attn_proj_kernel(x_ref, mod_ref, g_ref, wm_ref, wt_ref, c128_ref, s128_ref, c64_ref, s64lo_ref,
                      s64hi_ref, ct_ref, stlo_ref, sthi_ref, q_ref, k_ref, v_ref, qi_ref, kw_ref, k4_ref, v4_ref):
    h = _norm_mod(x_ref[0], g_ref[...], mod_ref[0, 1:2, :], mod_ref[0, 0:1, :]).astype(BF16)
    pm = _dot(h, wm_ref[...])
    c128 = c128_ref[...]
    s128 = s128_ref[...]
    for hh in range(N_HEADS):
        lo = hh * HEAD_DIM
        q_ref[0, :, lo:lo + HEAD_DIM] = _rope128(pm[:, lo:lo + HEAD_DIM], c128, s128) * Q_SCALE
    for hh in range(N_KV_HEADS):
        lo = hh * HEAD_DIM
        kh = _rope128(pm[:, Q_END + lo:Q_END + lo + HEAD_DIM], c128, s128)
        k_ref[0, :, lo:lo + HEAD_DIM] = kh
        k4_ref[0, :, hh, :] = kh
        v4_ref[0, :, hh, :] = pm[:, K_END + lo:K_END + lo + HEAD_DIM]
    v_ref[0] = pm[:, K_END:V_END]
    c64 = c64_ref[...]
    s64lo = s64lo_ref[...]
    s64hi = s64hi_ref[...]
    for cc in range(IDX_HEADS * IDX_DIM // LANES):
        lo = cc * LANES
        qi_ref[0, :, lo:lo + LANES] = _rope64(pm[:, V_END + lo:V_END + lo + LANES], c64, s64lo, s64hi)
    pt = _dot(h, wt_ref[...])
    kw_ref[0] = _rope64(pt, ct_ref[...], stlo_ref[...], sthi_ref[...])


def _rope_tables(t_len, past):
    pos = jnp.arange(past, past + t_len, dtype=jnp.int32).astype(F32)[:, None]

    def cs(d):
        half = d // 2
        inv = ROPE_THETA ** (-2.0 * jnp.arange(half, dtype=F32) / d)
        ang = pos * inv[None, :]
        return jnp.cos(ang), jnp.sin(ang)

    c, s = cs(HEAD_DIM)
    c128 = jnp.concatenate([c, c], axis=1)
    s128 = jnp.concatenate([-s, s], axis=1)
    c, s = cs(IDX_DIM)
    z = jnp.zeros_like(s)
    c64 = jnp.concatenate([c, c, c, c], axis=1)
    s64lo = jnp.concatenate([-s, z, -s, z], axis=1)
    s64hi = jnp.concatenate([z, s, z, s], axis=1)
    wscale = jnp.full((t_len, LANES - IDX_DIM), IDX_HEADS ** -0.5, F32)
    ct = jnp.concatenate([c, c, wscale], axis=1)
    stlo = jnp.concatenate([-s, z, z, z], axis=1)
    sthi = jnp.concatenate([z, s, z, z], axis=1)
    return c128, s128, c64, s64lo, s64hi, ct, stlo, sthi


def _attn_proj(x, mod, g, w_main, w_tail, tables, tm):
    b, t, d = x.shape
    tab_spec = pl.BlockSpec((tm, LANES), lambda bi, i: (i, 0))
    const2 = lambda bi, i: (0, 0)
    row3 = lambda bi, i: (bi, i, 0)
    return pl.pallas_call(
        _attn_proj_kernel,
        name="attn_proj",
        out_shape=(jax.ShapeDtypeStruct((b, t, Q_END), F32),
                   jax.ShapeDtypeStruct((b, t, K_END - Q_END), F32),
                   jax.ShapeDtypeStruct((b, t, V_END - K_END), F32),
                   jax.ShapeDtypeStruct((b, t, QI_END - V_END), F32),
                   jax.ShapeDtypeStruct((b, t, LANES), F32),
                   jax.ShapeDtypeStruct((b, t, N_KV_HEADS, HEAD_DIM), F32),
                   jax.ShapeDtypeStruct((b, t, N_KV_HEADS, HEAD_DIM), F32)),
        grid=(b, t // tm),
        in_specs=[pl.BlockSpec((1, tm, d), row3),
                  pl.BlockSpec((1, 6, d), lambda bi, i: (bi, 0, 0)),
                  pl.BlockSpec((1, d), const2),
                  _wspec((d, QI_END)),
                  _wspec((d, LANES))] + [tab_spec] * 8,
        out_specs=(pl.BlockSpec((1, tm, Q_END), row3),
                   pl.BlockSpec((1, tm, K_END - Q_END), row3),
                   pl.BlockSpec((1, tm, V_END - K_END), row3),
                   pl.BlockSpec((1, tm, QI_END - V_END), row3),
                   pl.BlockSpec((1, tm, LANES), row3),
                   pl.BlockSpec((1, tm, N_KV_HEADS, HEAD_DIM), lambda bi, i: (bi, i, 0, 0)),
                   pl.BlockSpec((1, tm, N_KV_HEADS, HEAD_DIM), lambda bi, i: (bi, i, 0, 0))),
        compiler_params=_cparams(("parallel", "parallel")),
    )(x, mod, g, w_main, w_tail, *tables)


HEADS_PER_DOT = 4


def _col_reduce(x, op):
    rows, cols = x.shape
    slab = 8 * SUBLANES
    if rows % slab == 0 and rows > slab:
        x = op(x.reshape(rows // slab, slab, cols), axis=0)
    return op(x, axis=0, keepdims=True)


def _attn_core_kernel(q_ref, qi_ref, kw_ref, k_ref, v_ref, ki_ref, x_ref, mod_ref, wo_ref, o_ref,
                      *, pos0, l_true, n_sel):
    tq = q_ref.shape[1]
    lp = k_ref.shape[1]
    i = pl.program_id(1)
    qpos = pos0 + i * tq + lax.broadcasted_iota(jnp.int32, (1, tq), 1)
    kidx = lax.broadcasted_iota(jnp.int32, (lp, tq), 0)
    adm = (lax.shift_right_logical(kidx, 6) <= lax.shift_right_logical(qpos, 6)) & (kidx < l_true)
    sel = adm if l_true <= n_sel else _topk_mask(qi_ref, kw_ref, ki_ref, adm, kidx, n_sel)

    outs = []
    for kv in range(N_KV_HEADS):
        kk = k_ref[0, :, kv * HEAD_DIM:(kv + 1) * HEAD_DIM].astype(BF16)
        vt = v_ref[0, :, kv * HEAD_DIM:(kv + 1) * HEAD_DIM].T
        vt_ones = jnp.concatenate([vt, jnp.ones((SUBLANES, lp), F32)], axis=0).astype(BF16)
        for g0 in range(0, GROUP, HEADS_PER_DOT):
            lo = (kv * GROUP + g0) * HEAD_DIM
            qcat = jnp.concatenate([q_ref[0, :, lo + n * HEAD_DIM:lo + (n + 1) * HEAD_DIM]
                                    for n in range(HEADS_PER_DOT)], axis=0).astype(BF16)
            sn = _dot_nt(kk, qcat)
            ps = []
            for n in range(HEADS_PER_DOT):
                s = jnp.where(sel, sn[:, n * tq:(n + 1) * tq], -jnp.inf)
                m = _col_reduce(s, jnp.max)
                ps.append(jnp.exp2(s - m).astype(BF16))
            on = _dot(vt_ones, jnp.concatenate(ps, axis=1))
            for n in range(HEADS_PER_DOT):
                outs.append(on[:HEAD_DIM, n * tq:(n + 1) * tq] / on[HEAD_DIM:HEAD_DIM + 1, n * tq:(n + 1) * tq])
    o = jnp.concatenate(outs, axis=0).T.astype(BF16)
    o_ref[0] = x_ref[0] + mod_ref[0, 2:3, :] * _dot(o, wo_ref[...])


def _topk_mask(qi_ref, kw_ref, ki_ref, adm, kidx, n_sel):
    lp, tq = kidx.shape
    kib = ki_ref[0].astype(BF16)
    wit = kw_ref[0].T
    score = jnp.zeros((lp, tq), F32)
    for h0 in range(0, IDX_HEADS, HEADS_PER_DOT):
        qcat = jnp.concatenate([qi_ref[0, :, (h0 + n) * IDX_DIM:(h0 + n + 1) * IDX_DIM]
                                for n in range(HEADS_PER_DOT)], axis=0).astype(BF16)
        lg = _dot_nt(kib, qcat)
        for n in range(HEADS_PER_DOT):
            w_row = wit[IDX_DIM + h0 + n:IDX_DIM + h0 + n + 1, :]
            score = score + w_row * jnp.maximum(lg[:, n * tq:(n + 1) * tq], 0.0)

    score = jnp.where(jnp.abs(score) < F32_TINY, 0.0, score)
    bits = lax.bitcast_convert_type(score, jnp.int32)
    key = bits ^ ((bits >> 31) & jnp.int32(0x7FFFFFFF))
    key = jnp.where(adm, key, jnp.int32(INT_MIN))
    kf = jnp.float32(n_sel)

    pack = 2 * SUBLANES
    assert lp % (SUBLANES * pack) == 0 and lp // pack <= 256
    hi = jnp.where(adm, lax.bitcast_convert_type(bits & jnp.int32(-65536), F32), -jnp.inf).astype(BF16)
    hi = hi.reshape(lp // (SUBLANES * pack), SUBLANES, pack, tq)
    one16 = jnp.ones((pack, tq), BF16)
    zero16 = jnp.zeros((pack, tq), BF16)

    def hi_body(it, t):
        cand = t + lax.shift_left(jnp.int32(1), 15 - it)
        ceff = jnp.where((cand > 0) & (cand < 128), 128, jnp.where((cand < 0) & (cand >= -128), 0, cand))
        cbits = lax.shift_left(ceff ^ ((ceff >> 15) & jnp.int32(0x7FFF)), 16)
        cval = jnp.broadcast_to(lax.bitcast_convert_type(cbits, F32), (pack, tq)).astype(BF16)
        ones = jnp.where(hi >= cval, one16, zero16)
        part = ones[0]
        for n in range(1, ones.shape[0]):
            part = part + ones[n]
        tot = part[0]
        for n in range(1, SUBLANES):
            tot = tot + part[n]
        cnt = jnp.sum(tot.astype(F32), axis=0, keepdims=True)
        return jnp.where(cnt >= kf, cand, t)

    thr_hi = lax.fori_loop(0, 16, hi_body, jnp.full((1, tq), -2 ** 15, jnp.int32))

    def lo_body(it, t):
        cand = t + lax.shift_left(jnp.int32(1), 15 - it)
        cnt = _col_reduce(jnp.where(key >= cand, 1.0, 0.0), jnp.sum)
        return jnp.where(cnt >= kf, cand, t)

    thr = lax.fori_loop(0, 16, lo_body, lax.shift_left(thr_hi, 16))
    gt = key > thr
    eq = (key == thr) & adm
    need = kf - _col_reduce(jnp.where(gt, 1.0, 0.0), jnp.sum)
    eqf = jnp.where(eq, 1.0, 0.0)
    excess = _col_reduce(eqf, jnp.sum) - need

    def tie_search():
        nbits = lp.bit_length()

        def body(it, m):
            cand = m + lax.shift_left(jnp.int32(1), nbits - 1 - it)
            f = _col_reduce(jnp.where(kidx < cand, eqf, 0.0), jnp.sum)
            return jnp.where(f < need, cand, m)

        return lax.fori_loop(0, nbits, body, jnp.zeros((1, tq), jnp.int32)) + 1

    jstar = lax.cond(jnp.max(excess) > 0.0, tie_search, lambda: jnp.full((1, tq), lp, jnp.int32))
    return gt | (eq & (kidx < jstar))


def _attn_core(q, qi, kw, k_all, v_all, ki_all, x, mod, w_out, *, past, row0, rows, n_sel, tq):
    b, t, d = x.shape
    l_true = past + row0 + rows
    lp = -(-l_true // LANES) * LANES
    blk0 = row0 // tq
    row3 = lambda bi, i: (bi, blk0 + i, 0)
    per_b = lambda bi, i: (bi, 0, 0)
    kern = functools.partial(_attn_core_kernel, pos0=past + row0, l_true=l_true, n_sel=n_sel)
    return pl.pallas_call(
        kern,
        name=f"attn_core_l{lp}",
        out_shape=jax.ShapeDtypeStruct((b, t, d), F32),
        input_output_aliases={6: 0},
        grid=(b, rows // tq),
        in_specs=[pl.BlockSpec((1, tq, Q_END), row3),
                  pl.BlockSpec((1, tq, QI_END - V_END), row3),
                  pl.BlockSpec((1, tq, LANES), row3),
                  pl.BlockSpec((1, lp, K_END - Q_END), per_b),
                  pl.BlockSpec((1, lp, V_END - K_END), per_b),
                  pl.BlockSpec((1, lp, IDX_DIM), per_b),
                  pl.BlockSpec((1, tq, d), row3),
                  pl.BlockSpec((1, 6, d), per_b),
                  _wspec((N_HEADS * HEAD_DIM, d))],
        out_specs=pl.BlockSpec((1, tq, d), row3),
        compiler_params=_cparams(("parallel", "parallel")),
    )(q, qi, kw, k_all, v_all, ki_all, x, mod, w_out)


def _conv_mixer_kernel(x_ref, mod_ref, g_ref, win_ref, cw_ref, wout_ref, left_ref, o_ref, st_ref, carry_ref):
    d = D_MODEL
    tm = x_ref.shape[1]

    @pl.when(pl.program_id(1) == 0)
    def _():
        carry_ref[...] = jnp.zeros_like(carry_ref)
        carry_ref[6:8, :] = left_ref[0]

    x = x_ref[0]
    h = _norm_mod(x, g_ref[...], mod_ref[0, 1:2, :], mod_ref[0, 0:1, :]).astype(BF16)
    gb = _dot(h, win_ref[:, 0:d])
    gc = _dot(h, win_ref[:, d:2 * d])
    u = gc * _dot(h, win_ref[:, 2 * d:3 * d])
    y = _causal_conv3(u, carry_ref[...], cw_ref[0:1, :], cw_ref[1:2, :], cw_ref[2:3, :])
    carry_ref[...] = u[tm - SUBLANES:tm, :]
    st_ref[0] = u[tm - 2:tm, :]
    o_ref[0] = x + mod_ref[0, 2:3, :] * _dot((gb * y).astype(BF16), wout_ref[...])


def _conv_mixer(x, mod, g, w_in, conv_w, w_out, left, tm):
    b, t, d = x.shape
    row3 = lambda bi, i: (bi, i, 0)
    per_b = lambda bi, i: (bi, 0, 0)
    const2 = lambda bi, i: (0, 0)
    return pl.pallas_call(
        _conv_mixer_kernel,
        name="conv_mixer",
        out_shape=(jax.ShapeDtypeStruct((b, t, d), F32), jax.ShapeDtypeStruct((b, 2, d), F32)),
        grid=(b, t // tm),
        in_specs=[pl.BlockSpec((1, tm, d), row3),
                  pl.BlockSpec((1, 6, d), per_b),
                  pl.BlockSpec((1, d), const2),
                  _wspec((d, 3 * d)),
                  pl.BlockSpec((3, d), const2),
                  _wspec((d, d)),
                  pl.BlockSpec((1, 2, d), per_b)],
        out_specs=(pl.BlockSpec((1, tm, d), row3), pl.BlockSpec((1, 2, d), per_b)),
        scratch_shapes=[pltpu.VMEM((SUBLANES, d), F32)],
        compiler_params=_cparams(("parallel", "arbitrary")),
    )(x, mod, g, w_in, conv_w, w_out, left)


FFN_COLS = 512
FFN_ROWS = 1024


def _ffn_kernel(x_ref, mod_ref, g_ref, wup_ref, cw_ref, cb_ref, wdn_ref, left_ref, gfin_ref, o_ref, st_ref,
                carry_ref, *, final_norm):
    tm = x_ref.shape[1]

    @pl.when(pl.program_id(1) == 0)
    def _():
        carry_ref[...] = jnp.zeros_like(carry_ref)
        carry_ref[6:8, :] = left_ref[0]

    x = x_ref[0]
    h = _norm_mod(x, g_ref[...], mod_ref[0, 4:5, :], mod_ref[0, 3:4, :]).astype(BF16)

    def up_cols(c):
        return _dot(h, wup_ref[:, c:c + FFN_COLS]), _dot(h, wup_ref[:, D_FF + c:D_FF + c + FFN_COLS])

    def conv_cols(u, lo):
        z = _causal_conv3(u, carry_ref[:, lo:lo + FFN_COLS], cw_ref[0:1, lo:lo + FFN_COLS],
                          cw_ref[1:2, lo:lo + FFN_COLS], cw_ref[2:3, lo:lo + FFN_COLS]) + cb_ref[:, lo:lo + FFN_COLS]
        carry_ref[:, lo:lo + FFN_COLS] = u[tm - SUBLANES:tm, :]
        st_ref[0, :, lo:lo + FFN_COLS] = u[tm - 2:tm, :]
        return z

    acc = jnp.zeros((tm, D_MODEL), F32)
    u_next = up_cols(0)
    for c in range(0, D_FF, FFN_COLS):
        u_gate, u_val = u_next
        if c + FFN_COLS < D_FF:
            u_next = up_cols(c + FFN_COLS)
        gate = conv_cols(u_gate, c)
        val = conv_cols(u_val, D_FF + c)
        act = (gate * jax.nn.sigmoid(gate)) * val
        acc = acc + _dot(act.astype(BF16), wdn_ref[c:c + FFN_COLS, :])
    y = x + mod_ref[0, 5:6, :] * acc
    if final_norm:
        ms = jnp.mean(y * y, axis=-1, keepdims=True)
        y = y * lax.rsqrt(ms + NORM_EPS) * gfin_ref[...]
    o_ref[0] = y


def _ffn(x, mod, g, w_up, conv_w, conv_b, w_down, left, g_final, tm, final_norm):
    b, t, d = x.shape
    row3 = lambda bi, i: (bi, i, 0)
    per_b = lambda bi, i: (bi, 0, 0)
    const2 = lambda bi, i: (0, 0)
    kern = functools.partial(_ffn_kernel, final_norm=final_norm)
    return pl.pallas_call(
        kern,
        name="conv_ffn",
        out_shape=(jax.ShapeDtypeStruct((b, t, d), F32), jax.ShapeDtypeStruct((b, 2, 2 * D_FF), F32)),
        grid=(b, t // tm),
        in_specs=[pl.BlockSpec((1, tm, d), row3),
                  pl.BlockSpec((1, 6, d), per_b),
                  pl.BlockSpec((1, d), const2),
                  _wspec((d, 2 * D_FF)),
                  pl.BlockSpec((3, 2 * D_FF), const2),
                  pl.BlockSpec((1, 2 * D_FF), const2),
                  _wspec((D_FF, d)),
                  pl.BlockSpec((1, 2, 2 * D_FF), per_b),
                  pl.BlockSpec((1, d), const2)],
        out_specs=(pl.BlockSpec((1, tm, d), row3), pl.BlockSpec((1, 2, 2 * D_FF), per_b)),
        scratch_shapes=[pltpu.VMEM((SUBLANES, 2 * D_FF), F32)],
        compiler_params=_cparams(("parallel", "arbitrary")),
    )(x, mod, g, w_up, conv_w, conv_b, w_down, left, g_final)


def _softplus(z):
    return jnp.maximum(z, 0.0) + jnp.log1p(jnp.exp(-jnp.abs(z)))


def _rwkv_proj_kernel(x_ref, mod_ref, g_ref, sp_ref, mix_ref, wrkv_ref, w1_ref, w2_ref, w0_ref, a1_ref, a2_ref,
                      a0_ref, g1_ref, g2_ref, r_ref, k_ref, v_ref, w_ref, a_ref, gg_ref, sh_ref, carry_ref):
    tm = x_ref.shape[1]

    @pl.when(pl.program_id(1) == 0)
    def _():
        carry_ref[...] = jnp.zeros_like(carry_ref)
        carry_ref[7:8, :] = sp_ref[0]

    h = _norm_mod(x_ref[0], g_ref[...], mod_ref[0, 1:2, :], mod_ref[0, 0:1, :])
    (hs,) = _shifted(h, carry_ref[...], 1)
    carry_ref[...] = h[tm - SUBLANES:tm, :]
    sh_ref[0] = h[tm - 1:tm, :]
    xx = hs - h

    def mixed(j):
        return (h + xx * mix_ref[j:j + 1, :]).astype(BF16)

    r_ref[0] = _dot(mixed(0), wrkv_ref[0])
    k_ref[0] = _dot(mixed(2), wrkv_ref[1])
    v_ref[0] = _dot(mixed(3), wrkv_ref[2])
    lw = _dot(jnp.tanh(_dot(mixed(1), w1_ref[...])).astype(BF16), w2_ref[...])
    w_log = -_softplus(-(w0_ref[...] + lw)) - 0.5
    w_ref[0] = jnp.exp(-jnp.exp(w_log))
    a_ref[0] = jax.nn.sigmoid(a0_ref[...] + _dot(_dot(mixed(4), a1_ref[...]).astype(BF16), a2_ref[...]))
    gg_ref[0] = _dot(jax.nn.sigmoid(_dot(mixed(5), g1_ref[...])).astype(BF16), g2_ref[...])


def _rwkv_proj(x, mod, g, shift_prev, mix, w_rkv, w1, w2, w0, a1, a2, a0, g1, g2, tm):
    b, t, d = x.shape
    row3 = lambda bi, i: (bi, i, 0)
    per_b = lambda bi, i: (bi, 0, 0)
    const2 = lambda bi, i: (0, 0)
    big = jax.ShapeDtypeStruct((b, t, d), F32)
    lora = w1.shape[1]
    gl = g1.shape[1]
    return pl.pallas_call(
        _rwkv_proj_kernel,
        name="rwkv_proj",
        out_shape=(big,) * 6 + (jax.ShapeDtypeStruct((b, 1, d), F32),),
        grid=(b, t // tm),
        in_specs=[pl.BlockSpec((1, tm, d), row3),
                  pl.BlockSpec((1, 6, d), per_b),
                  pl.BlockSpec((1, d), const2),
                  pl.BlockSpec((1, 1, d), per_b),
                  pl.BlockSpec((6, d), const2),
                  _wspec((3, d, d)),
                  pl.BlockSpec((d, lora), const2),
                  pl.BlockSpec((lora, d), const2),
                  pl.BlockSpec((1, d), const2),
                  pl.BlockSpec((d, lora), const2),
                  pl.BlockSpec((lora, d), const2),
                  pl.BlockSpec((1, d), const2),
                  pl.BlockSpec((d, gl), const2),
                  pl.BlockSpec((gl, d), const2)],
        out_specs=(pl.BlockSpec((1, tm, d), row3),) * 6 + (pl.BlockSpec((1, 1, d), per_b),),
        scratch_shapes=[pltpu.VMEM((SUBLANES, d), F32)],
        compiler_params=_cparams(("parallel", "arbitrary")),
    )(x, mod, g, shift_prev, mix, w_rkv, w1, w2, w0, a1, a2, a0, g1, g2)


SCAN_STEPS = 64
def _rwkv_scan_kernel(r_ref, k_ref, v_ref, w_ref, a_ref, kk_ref, ka_ref, rk_ref, lg_ref, lb_ref, s0_ref,
                      y_ref, sfin_ref, s_ref, bc_ref):
    tc = r_ref.shape[0]
    n = RW_HEAD

    @pl.when(pl.program_id(1) == 0)
    def _():
        s_ref[...] = s0_ref[...]

    def unit_kk(t):
        kkr = k_ref[t] * kk_ref[...]
        nrm = jnp.sqrt(jnp.sum(kkr * kkr, axis=0, keepdims=True))
        return kkr / jnp.maximum(nrm, 1e-12)

    def step(t, carry, has_next):
        cum_prev, sa, kk = carry
        kt = k_ref[t]
        at = a_ref[t]
        rt = r_ref[t]
        vt = v_ref[t]
        cum = cum_prev * w_ref[t]
        inv = 1.0 / cum
        kmod = kt * (1.0 + (at - 1.0) * ka_ref[...])
        bc_ref[0] = (kk * at) * inv
        bc_ref[1] = kmod * inv
        bc_ref[2] = rt * cum
        if has_next:
            kk_next = unit_kk(t + 1)
            bc_ref[3] = -kk_next * cum
        y = jnp.zeros((n, LANES), F32)
        sa_next = jnp.zeros((n, LANES), F32)
        for j in range(n):
            sj = s_ref[j] + sa * bc_ref[0, j:j + 1, :] + vt * bc_ref[1, j:j + 1, :]
            s_ref[j] = sj
            y = y + sj * bc_ref[2, j:j + 1, :]
            if has_next:
                sa_next = sa_next + sj * bc_ref[3, j:j + 1, :]
        mu = jnp.mean(y, axis=0, keepdims=True)
        yc = y - mu
        var = jnp.mean(yc * yc, axis=0, keepdims=True)
        yn = (yc * lax.rsqrt(var + RW_GN_EPS)) * lg_ref[...] + lb_ref[...]
        bonus = jnp.sum(rt * kmod * rk_ref[...], axis=0, keepdims=True) * vt
        y_ref[t] = yn + bonus
        if has_next:
            return cum, sa_next, kk_next
        return cum

    kk0 = unit_kk(0)
    bc_ref[3] = -kk0
    sa0 = jnp.zeros((n, LANES), F32)
    for j in range(n):
        sa0 = sa0 + s_ref[j] * bc_ref[3, j:j + 1, :]
    carry = (jnp.ones((n, LANES), F32), sa0, kk0)
    carry = lax.fori_loop(0, tc - 1, lambda t, c: step(t, c, True), carry)
    bc_ref[0] = step(tc - 1, carry, False)
    for j in range(n):
        s_ref[j] = s_ref[j] * bc_ref[0, j:j + 1, :]
    sfin_ref[...] = s_ref[...]


def _rwkv_scan(r, k, v, w, a, kk_t, ka_t, rk_t, lg_t, lb_t, s0, tc):
    t, n, bh = r.shape
    seq = pl.BlockSpec((tc, n, LANES), lambda lb, ti: (ti, 0, lb))
    par = pl.BlockSpec((n, LANES), lambda lb, ti: (0, lb))
    st = pl.BlockSpec((n, n, LANES), lambda lb, ti: (0, 0, lb))
    return pl.pallas_call(
        _rwkv_scan_kernel,
        name="rwkv_scan",
        out_shape=(jax.ShapeDtypeStruct((t, n, bh), F32), jax.ShapeDtypeStruct((n, n, bh), F32)),
        grid=(bh // LANES, t // tc),
        in_specs=[seq] * 5 + [par] * 5 + [st],
        out_specs=(seq, st),
        scratch_shapes=[pltpu.VMEM((n, n, LANES), F32), pltpu.VMEM((5, n, LANES), F32)],
        compiler_params=_cparams(("parallel", "arbitrary")),
    )(r, k, v, w, a, kk_t, ka_t, rk_t, lg_t, lb_t, s0)


def _rwkv_out_kernel(y_ref, gg_ref, x_ref, mod_ref, wo_ref, o_ref):
    o_ref[0] = x_ref[0] + mod_ref[0, 2:3, :] * _dot((y_ref[0] * gg_ref[0]).astype(BF16), wo_ref[...])


def _rwkv_out(y, gg, x, mod, w_o, tm):
    b, t, d = x.shape
    row3 = lambda bi, i: (bi, i, 0)
    return pl.pallas_call(
        _rwkv_out_kernel,
        name="rwkv_out",
        out_shape=jax.ShapeDtypeStruct((b, t, d), F32),
        grid=(b, t // tm),
        in_specs=[pl.BlockSpec((1, tm, d), row3), pl.BlockSpec((1, tm, d), row3), pl.BlockSpec((1, tm, d), row3),
                  pl.BlockSpec((1, 6, d), lambda bi, i: (bi, 0, 0)),
                  _wspec((d, d))],
        out_specs=pl.BlockSpec((1, tm, d), row3),
        compiler_params=_cparams(("parallel", "parallel")),
    )(y, gg, x, mod, w_o)


def _pad_rows(a, rows):
    return jnp.pad(a, ((0, 0), (0, rows - a.shape[1]), (0, 0)))


def _trunk(x, mods, states, ffn_left, wts, past):
    b, t, d = x.shape
    tm = min(t, 512)
    tq = min(t, 256)
    g_norm = wts['g_norm']
    new_states = []
    new_ffn = []
    tables = _rope_tables(t, past)
    for i in range(DEPTH):
        mod = mods[i]
        kind = i % 3
        g_a = g_norm[i, 0][None, :]
        if kind == 0:
            j = i // 3
            past_k, past_v, past_ki = states[i]
            q, k, v, qi, kw, k4, v4 = _attn_proj(x, mod, g_a, wts['attn_w_main'][j], wts['attn_w_tail'][j],
                                                 tables, tm)
            ki = kw[:, :, :IDX_DIM]
            l_all = past + t
            if past:
                lp = -(-l_all // LANES) * LANES
                kvd = N_KV_HEADS * HEAD_DIM
                k_all = _pad_rows(jnp.concatenate([past_k.reshape(b, past, kvd), k], axis=1), lp)
                v_all = _pad_rows(jnp.concatenate([past_v.reshape(b, past, kvd), v], axis=1), lp)
                ki_all = _pad_rows(jnp.concatenate([past_ki, ki], axis=1), lp)
            else:
                k_all, v_all, ki_all = k, v, ki
            for row0 in range(0, t, tq):
                x = _attn_core(q, qi, kw, k_all, v_all, ki_all, x, mod, wts['attn_w_out'][j], past=past,
                               row0=row0, rows=tq, n_sel=min(TOPK_MAX, l_all // 4), tq=tq)
            new_states.append((k4, v4, ki))
        elif kind == 1:
            x, conv_st = _conv_mixer(x, mod, g_a, wts['sc_w_in'], wts['sc_conv_w'], wts['sc_w_out'],
                                     states[i][0], tm)
            new_states.append((conv_st,))
        else:
            shift_prev, wkv0 = states[i]
            r, k, v, w, a, gg, shift_new = _rwkv_proj(
                x, mod, g_a, shift_prev, wts['rw_mix'], wts['rw_w_rkv'], wts['rw_w1'], wts['rw_w2'], wts['rw_w0'],
                wts['rw_a1'], wts['rw_a2'], wts['rw_a0'], wts['rw_g1'], wts['rw_g2'], tm)
            bh = b * RW_HEADS

            def to_scan(z):
                return z.reshape(b, t, RW_HEADS, RW_HEAD).transpose(1, 3, 0, 2).reshape(t, RW_HEAD, bh)

            def head_tile(p):
                return jnp.tile(p.reshape(RW_HEADS, RW_HEAD).T, (1, b))

            s0 = wkv0.transpose(3, 2, 0, 1).reshape(RW_HEAD, RW_HEAD, bh)
            y, s_fin = _rwkv_scan(to_scan(r), to_scan(k), to_scan(v), to_scan(w), to_scan(a),
                                  head_tile(wts['rw_k_k']), head_tile(wts['rw_k_a']), head_tile(wts['rw_r_k']),
                                  head_tile(wts['rw_ln_g']), head_tile(wts['rw_ln_b']), s0, min(t, SCAN_STEPS))
            y = y.reshape(t, RW_HEAD, b, RW_HEADS).transpose(2, 0, 3, 1).reshape(b, t, d)
            wkv_new = s_fin.reshape(RW_HEAD, RW_HEAD, b, RW_HEADS).transpose(2, 3, 1, 0)
            x = _rwkv_out(y, gg, x, mod, wts['rw_w_o'], tm)
            new_states.append((shift_new, wkv_new))
        x, f_st = _ffn(x, mod, g_norm[i, 1][None, :], wts['ffn_w_up'][i], wts['ffn_conv_w'][i],
                       wts['ffn_conv_b'][i][None, :], wts['ffn_w_down'][i], ffn_left[i],
                       wts['g_final'][None, :], min(t, FFN_ROWS), i == DEPTH - 1)
        new_ffn.append(f_st)
    return x, new_states, jnp.stack(new_ffn)


def _pad_cols(w, cols):
    return jnp.pad(w, ((0, 0), (0, cols - w.shape[1])))


def _pad_rows2(w, rows):
    return jnp.pad(w, ((0, rows - w.shape[0]), (0, 0)))


def kernel(x_prompt, x_sample, c_prompt, c_sample, cache_k_0, cache_v_0, cache_kidx_0, state_conv_1, state_shift_2, state_wkv_2, cache_k_3, cache_v_3, cache_kidx_3, state_ffn_conv, w_mod, b_mod, g_norm, g_final, attn_w_in, attn_w_out, sc_w_in, sc_conv_w, sc_w_out, rw_mix, rw_w_rkv, rw_w_o, rw_w0, rw_w1, rw_w2, rw_a0, rw_a1, rw_a2, rw_g1, rw_g2, rw_k_k, rw_k_a, rw_r_k, rw_ln_g, rw_ln_b, ffn_w_up, ffn_conv_w, ffn_conv_b, ffn_w_down):
    d = D_MODEL
    bp = x_prompt.shape[0]
    bs = x_sample.shape[0]
    dt = x_prompt.dtype

    wts = _prep_weights(g_norm, g_final, attn_w_in, attn_w_out, sc_w_in, sc_conv_w, sc_w_out, rw_mix, rw_w_rkv,
                        rw_w_o, rw_w0, rw_w1, rw_w2, rw_a0, rw_a1, rw_a2, rw_g1, rw_g2, rw_k_k, rw_k_a, rw_r_k,
                        rw_ln_g, rw_ln_b, ffn_w_up, ffn_conv_w, ffn_conv_b, ffn_w_down)

    mods = _mod_all(jnp.concatenate([c_prompt, c_sample], axis=0), w_mod, b_mod)
    mods_p = mods[:, :bp].reshape(DEPTH, bp, 6, d)
    mods_s = mods[:, bp:].reshape(DEPTH, bs, 6, d)

    ffn0 = jnp.zeros((DEPTH, bp, 2, 2 * D_FF), dt)
    y_p, st_p, ffn_p = _trunk(x_prompt, mods_p, _empty_states(bp, dt), ffn0, wts, 0)
    sample_states = [(cache_k_0, cache_v_0, cache_kidx_0), (state_conv_1,), (state_shift_2, state_wkv_2),
                     (cache_k_3, cache_v_3, cache_kidx_3)]
    y_s, st_s, ffn_s = _trunk(x_sample, mods_s, sample_states, state_ffn_conv, wts, cache_k_0.shape[1])
    (k0_p, v0_p, ki0_p), (conv1_p,), (shift2_p, wkv2_p), (k3_p, v3_p, ki3_p) = st_p
    (k0_s, v0_s, ki0_s), (conv1_s,), (shift2_s, wkv2_s), (k3_s, v3_s, ki3_s) = st_s
    return (y_p, y_s,
            k0_p, v0_p, ki0_p, conv1_p, shift2_p, wkv2_p, k3_p, v3_p, ki3_p, ffn_p,
            k0_s, v0_s, ki0_s, conv1_s, shift2_s, wkv2_s, k3_s, v3_s, ki3_s, ffn_s)


def _empty_states(b, dt):
    d = D_MODEL
    kv = (jnp.zeros((b, 0, N_KV_HEADS, HEAD_DIM), dt), jnp.zeros((b, 0, N_KV_HEADS, HEAD_DIM), dt),
          jnp.zeros((b, 0, IDX_DIM), dt))
    return [kv, (jnp.zeros((b, 2, d), dt),),
            (jnp.zeros((b, 1, d), dt), jnp.zeros((b, RW_HEADS, RW_HEAD, RW_HEAD), dt)), kv]


def _prep_weights(g_norm, g_final, attn_w_in, attn_w_out, sc_w_in, sc_conv_w, sc_w_out, rw_mix, rw_w_rkv, rw_w_o,
                  rw_w0, rw_w1, rw_w2, rw_a0, rw_a1, rw_a2, rw_g1, rw_g2, rw_k_k, rw_k_a, rw_r_k, rw_ln_g, rw_ln_b,
                  ffn_w_up, ffn_conv_w, ffn_conv_b, ffn_w_down):
    return {
        'g_norm': g_norm, 'g_final': g_final,
        'attn_w_main': attn_w_in[:, :, :QI_END].astype(BF16),
        'attn_w_tail': jnp.pad(attn_w_in[:, :, QI_END:], ((0, 0), (0, 0), (0, LANES - (WI_END - QI_END)))).astype(BF16),
        'attn_w_out': attn_w_out.astype(BF16),
        'sc_w_in': sc_w_in.astype(BF16), 'sc_conv_w': sc_conv_w, 'sc_w_out': sc_w_out.astype(BF16),
        'rw_mix': rw_mix, 'rw_w_rkv': rw_w_rkv.astype(BF16), 'rw_w_o': rw_w_o.astype(BF16),
        'rw_w0': rw_w0[None, :], 'rw_w1': _pad_cols(rw_w1, LANES).astype(BF16),
        'rw_w2': _pad_rows2(rw_w2, LANES).astype(BF16),
        'rw_a0': rw_a0[None, :], 'rw_a1': _pad_cols(rw_a1, LANES).astype(BF16),
        'rw_a2': _pad_rows2(rw_a2, LANES).astype(BF16),
        'rw_g1': rw_g1.astype(BF16), 'rw_g2': rw_g2.astype(BF16),
        'rw_k_k': rw_k_k, 'rw_k_a': rw_k_a, 'rw_r_k': rw_r_k, 'rw_ln_g': rw_ln_g, 'rw_ln_b': rw_ln_b,
        'ffn_w_up': ffn_w_up.astype(BF16), 'ffn_conv_w': ffn_conv_w, 'ffn_conv_b': ffn_conv_b,
        'ffn_w_down': ffn_w_down.astype(BF16),
    }
```

```python
import functools

import jax
import jax.numpy as jnp
from jax import lax
from jax.experimental import pallas as pl
from jax.experimental.pallas import tpu as pltpu

D_MODEL = 1024
DEPTH = 4
CHUNK = 64
N_HEADS = 8
HEAD_DIM = 128
N_KV_HEADS = 2
GROUP = N_HEADS // N_KV_HEADS
IDX_HEADS = 8
IDX_DIM = 64
TOPK_MAX = 256
ROPE_THETA = 10000.0
RW_HEAD = 64
RW_HEADS = D_MODEL // RW_HEAD
RW_GN_EPS = 64e-5
D_FF = 2 * D_MODEL
NORM_EPS = 1e-6

LANES = 128
SUBLANES = 8
VMEM_LIMIT = 52 * 1024 * 1024
INT_MIN = -2 ** 31
F32_TINY = 1.1754943508222875e-38

F32 = jnp.float32
BF16 = jnp.bfloat16

Q_END = N_HEADS * HEAD_DIM
K_END = Q_END + N_KV_HEADS * HEAD_DIM
V_END = K_END + N_KV_HEADS * HEAD_DIM
QI_END = V_END + IDX_HEADS * IDX_DIM
KI_END = QI_END + IDX_DIM
WI_END = KI_END + IDX_HEADS
Q_SCALE = HEAD_DIM ** -0.5 * 1.4426950408889634


def _cparams(sem):
    return pltpu.CompilerParams(dimension_semantics=sem, vmem_limit_bytes=VMEM_LIMIT)


def _wspec(shape):
    return pl.BlockSpec(shape, lambda bi, i: (0,) * len(shape), pipeline_mode=pl.Buffered(1))


def _dot(a, b):
    return jnp.dot(a, b, preferred_element_type=F32)


def _dot_nt(a, b):
    return lax.dot_general(a, b, (((1,), (1,)), ((), ())), preferred_element_type=F32)


def _dot_nt_split(a, b):
    half = a.shape[0] // 2
    return jnp.concatenate([_dot_nt(a[:half], b), _dot_nt(a[half:], b)], axis=0)


def _dot_ksplit(a, b):
    half = a.shape[1] // 2
    return _dot(a[:, :half], b[:half]) + _dot(a[:, half:], b[half:])


def _norm_mod(x, g, scale, shift):
    ms = jnp.mean(x * x, axis=-1, keepdims=True)
    return (x * lax.rsqrt(ms + NORM_EPS) * g) * (1.0 + scale) + shift


def _shifted(u, hist, n):
    tm, c = u.shape
    tiles = jnp.concatenate([hist, u], axis=0).reshape(tm // SUBLANES + 1, SUBLANES, c)
    row = lax.broadcasted_iota(jnp.int32, (1, SUBLANES, 1), 1)
    outs = []
    for s in range(n, 0, -1):
        rot = pltpu.roll(tiles, s, 1)
        outs.append(jnp.where(row < s, rot[:-1], rot[1:]).reshape(tm, c))
    return outs


def _causal_conv3(u, hist, w0, w1, w2):
    s2, s1 = _shifted(u, hist, 2)
    return w0 * s2 + w1 * s1 + w2 * u


def _mod_kernel(c_ref, w_ref, b_ref, o_ref):
    o_ref[0] = _dot(c_ref[...].astype(BF16), w_ref[0].astype(BF16)) + b_ref[0]


def _mod_all(c_all, w_mod, b_mod):
    nb = c_all.shape[0]
    d = D_MODEL
    return pl.pallas_call(
        _mod_kernel,
        name="adaln_mod",
        out_shape=jax.ShapeDtypeStruct((DEPTH, nb, 6 * d), F32),
        grid=(DEPTH, 6),
        in_specs=[pl.BlockSpec((nb, d), lambda l, n: (0, 0)),
                  pl.BlockSpec((1, d, d), lambda l, n: (l, 0, n)),
                  pl.BlockSpec((1, 1, d), lambda l, n: (l, 0, n))],
        out_specs=pl.BlockSpec((1, nb, d), lambda l, n: (l, 0, n)),
        compiler_params=_cparams(("parallel", "parallel")),
    )(c_all, w_mod, b_mod.reshape(DEPTH, 1, 6 * d))


def _rope128(seg, cos, sin_signed):
    return seg * cos + pltpu.roll(seg, HEAD_DIM // 2, 1) * sin_signed


def _rope64(seg, cos, sin_lo, sin_hi):
    return seg * cos + pltpu.roll(seg, LANES - IDX_DIM // 2, 1) * sin_lo + pltpu.roll(seg, IDX_DIM // 2, 1) * sin_hi


def _attn_proj_kernel(x_ref, mod_ref, g_ref, wm_ref, wt_ref, c128_ref, s128_ref, c64_ref, s64lo_ref,
                      s64hi_ref, ct_ref, stlo_ref, sthi_ref, q_ref, k_ref, v_ref, qi_ref, kw_ref, k4_ref, v4_ref):
    h = _norm_mod(x_ref[0], g_ref[...], mod_ref[0, 1:2, :], mod_ref[0, 0:1, :]).astype(BF16)
    pm = _dot(h, wm_ref[...])
    c128 = c128_ref[...]
    s128 = s128_ref[...]
    for hh in range(N_HEADS):
        lo = hh * HEAD_DIM
        q_ref[0, :, lo:lo + HEAD_DIM] = _rope128(pm[:, lo:lo + HEAD_DIM], c128, s128) * Q_SCALE
    for hh in range(N_KV_HEADS):
        lo = hh * HEAD_DIM
        kh = _rope128(pm[:, Q_END + lo:Q_END + lo + HEAD_DIM], c128, s128)
        k_ref[0, :, lo:lo + HEAD_DIM] = kh
        k4_ref[0, :, hh, :] = kh
        v4_ref[0, :, hh, :] = pm[:, K_END + lo:K_END + lo + HEAD_DIM]
    v_ref[0] = pm[:, K_END:V_END]
    c64 = c64_ref[...]
    s64lo = s64lo_ref[...]
    s64hi = s64hi_ref[...]
    for cc in range(IDX_HEADS * IDX_DIM // LANES):
        lo = cc * LANES
        qi_ref[0, :, lo:lo + LANES] = _rope64(pm[:, V_END + lo:V_END + lo + LANES], c64, s64lo, s64hi)
    pt = _dot(h, wt_ref[...])
    kw_ref[0] = _rope64(pt, ct_ref[...], stlo_ref[...], sthi_ref[...])


def _rope_tables(t_len, past):
    pos = jnp.arange(past, past + t_len, dtype=jnp.int32).astype(F32)[:, None]

    def cs(d):
        half = d // 2
        inv = ROPE_THETA ** (-2.0 * jnp.arange(half, dtype=F32) / d)
        ang = pos * inv[None, :]
        return jnp.cos(ang), jnp.sin(ang)

    c, s = cs(HEAD_DIM)
    c128 = jnp.concatenate([c, c], axis=1)
    s128 = jnp.concatenate([-s, s], axis=1)
    c, s = cs(IDX_DIM)
    z = jnp.zeros_like(s)
    c64 = jnp.concatenate([c, c, c, c], axis=1)
    s64lo = jnp.concatenate([-s, z, -s, z], axis=1)
    s64hi = jnp.concatenate([z, s, z, s], axis=1)
    wscale = jnp.full((t_len, LANES - IDX_DIM), IDX_HEADS ** -0.5, F32)
    ct = jnp.concatenate([c, c, wscale], axis=1)
    stlo = jnp.concatenate([-s, z, z, z], axis=1)
    sthi = jnp.concatenate([z, s, z, z], axis=1)
    return c128, s128, c64, s64lo, s64hi, ct, stlo, sthi


def _attn_proj(x, mod, g, w_main, w_tail, tables, tm):
    b, t, d = x.shape
    tab_spec = pl.BlockSpec((tm, LANES), lambda bi, i: (i, 0))
    const2 = lambda bi, i: (0, 0)
    row3 = lambda bi, i: (bi, i, 0)
    return pl.pallas_call(
        _attn_proj_kernel,
        name="attn_proj",
        out_shape=(jax.ShapeDtypeStruct((b, t, Q_END), F32),
                   jax.ShapeDtypeStruct((b, t, K_END - Q_END), F32),
                   jax.ShapeDtypeStruct((b, t, V_END - K_END), F32),
                   jax.ShapeDtypeStruct((b, t, QI_END - V_END), F32),
                   jax.ShapeDtypeStruct((b, t, LANES), F32),
                   jax.ShapeDtypeStruct((b, t, N_KV_HEADS, HEAD_DIM), F32),
                   jax.ShapeDtypeStruct((b, t, N_KV_HEADS, HEAD_DIM), F32)),
        grid=(b, t // tm),
        in_specs=[pl.BlockSpec((1, tm, d), row3),
                  pl.BlockSpec((1, 6, d), lambda bi, i: (bi, 0, 0)),
                  pl.BlockSpec((1, d), const2),
                  _wspec((d, QI_END)),
                  _wspec((d, LANES))] + [tab_spec] * 8,
        out_specs=(pl.BlockSpec((1, tm, Q_END), row3),
                   pl.BlockSpec((1, tm, K_END - Q_END), row3),
                   pl.BlockSpec((1, tm, V_END - K_END), row3),
                   pl.BlockSpec((1, tm, QI_END - V_END), row3),
                   pl.BlockSpec((1, tm, LANES), row3),
                   pl.BlockSpec((1, tm, N_KV_HEADS, HEAD_DIM), lambda bi, i: (bi, i, 0, 0)),
                   pl.BlockSpec((1, tm, N_KV_HEADS, HEAD_DIM), lambda bi, i: (bi, i, 0, 0))),
        compiler_params=_cparams(("parallel", "parallel")),
    )(x, mod, g, w_main, w_tail, *tables)


HEADS_PER_DOT = 4


def _col_reduce(x, op):
    rows, cols = x.shape
    slab = 8 * SUBLANES
    if rows % slab == 0 and rows > slab:
        x = op(x.reshape(rows // slab, slab, cols), axis=0)
    return op(x, axis=0, keepdims=True)


def _attn_core_kernel(q_ref, qi_ref, kw_ref, k_ref, v_ref, ki_ref, x_ref, mod_ref, wo_ref, *out_refs,
                      pos0, l_true, n_sel):
    o_ref = out_refs[-1]
    tq = q_ref.shape[1]
    lp = k_ref.shape[1]
    i = pl.program_id(1)
    qpos = pos0 + i * tq + lax.broadcasted_iota(jnp.int32, (1, tq), 1)
    kidx = lax.broadcasted_iota(jnp.int32, (lp, tq), 0)
    adm = (lax.shift_right_logical(kidx, 6) <= lax.shift_right_logical(qpos, 6)) & (kidx < l_true)
    sel = adm if l_true <= n_sel else _topk_mask(qi_ref, kw_ref, ki_ref, adm, kidx, n_sel)

    outs = []
    for kv in range(N_KV_HEADS):
        kk = k_ref[0, :, kv * HEAD_DIM:(kv + 1) * HEAD_DIM].astype(BF16)
        vt = v_ref[0, :, kv * HEAD_DIM:(kv + 1) * HEAD_DIM].T
        vt_ones = jnp.concatenate([vt, jnp.ones((SUBLANES, lp), F32)], axis=0).astype(BF16)
        for g0 in range(0, GROUP, HEADS_PER_DOT):
            lo = (kv * GROUP + g0) * HEAD_DIM
            qcat = jnp.concatenate([q_ref[0, :, lo + n * HEAD_DIM:lo + (n + 1) * HEAD_DIM]
                                    for n in range(HEADS_PER_DOT)], axis=0).astype(BF16)
            sn = _dot_nt(kk, qcat)
            ps = []
            for n in range(HEADS_PER_DOT):
                s = jnp.where(sel, sn[:, n * tq:(n + 1) * tq], -jnp.inf)
                m = _col_reduce(s, jnp.max)
                ps.append(jnp.exp2(s - m).astype(BF16))
            on = _dot(vt_ones, jnp.concatenate(ps, axis=1))
            for n in range(HEADS_PER_DOT):
                outs.append(on[:HEAD_DIM, n * tq:(n + 1) * tq] / on[HEAD_DIM:HEAD_DIM + 1, n * tq:(n + 1) * tq])
    o = jnp.concatenate(outs, axis=0).T.astype(BF16)
    o_ref[0] = x_ref[0] + mod_ref[0, 2:3, :] * _dot(o, wo_ref[...])


def _topk_mask(qi_ref, kw_ref, ki_ref, adm, kidx, n_sel):
    lp, tq = kidx.shape
    kib = ki_ref[0].astype(BF16)
    wit = kw_ref[0].T
    score = jnp.zeros((lp, tq), F32)
    for h0 in range(0, IDX_HEADS, HEADS_PER_DOT):
        qcat = jnp.concatenate([qi_ref[0, :, (h0 + n) * IDX_DIM:(h0 + n + 1) * IDX_DIM]
                                for n in range(HEADS_PER_DOT)], axis=0).astype(BF16)
        lg = _dot_nt(kib, qcat)
        for n in range(HEADS_PER_DOT):
            w_row = wit[IDX_DIM + h0 + n:IDX_DIM + h0 + n + 1, :]
            score = score + w_row * jnp.maximum(lg[:, n * tq:(n + 1) * tq], 0.0)

    score = jnp.where(jnp.abs(score) < F32_TINY, 0.0, score)
    bits = lax.bitcast_convert_type(score, jnp.int32)
    key = bits ^ ((bits >> 31) & jnp.int32(0x7FFFFFFF))
    key = jnp.where(adm, key, jnp.int32(INT_MIN))
    kf = jnp.float32(n_sel)

    pack = 2 * SUBLANES
    assert lp % (SUBLANES * pack) == 0 and lp // pack <= 256
    hi = jnp.where(adm, lax.bitcast_convert_type(bits & jnp.int32(-65536), F32), -jnp.inf).astype(BF16)
    hi = hi.reshape(lp // (SUBLANES * pack), SUBLANES, pack, tq)
    one16 = jnp.ones((pack, tq), BF16)
    zero16 = jnp.zeros((pack, tq), BF16)

    def hi_body(it, t):
        cand = t + lax.shift_left(jnp.int32(1), 15 - it)
        ceff = jnp.where((cand > 0) & (cand < 128), 128, jnp.where((cand < 0) & (cand >= -128), 0, cand))
        cbits = lax.shift_left(ceff ^ ((ceff >> 15) & jnp.int32(0x7FFF)), 16)
        cval = jnp.broadcast_to(lax.bitcast_convert_type(cbits, F32), (pack, tq)).astype(BF16)
        ones = jnp.where(hi >= cval, one16, zero16)
        part = ones[0]
        for n in range(1, ones.shape[0]):
            part = part + ones[n]
        tot = part[0]
        for n in range(1, SUBLANES):
            tot = tot + part[n]
        cnt = jnp.sum(tot.astype(F32), axis=0, keepdims=True)
        return jnp.where(cnt >= kf, cand, t)

    thr_hi = lax.fori_loop(0, 16, hi_body, jnp.full((1, tq), -2 ** 15, jnp.int32))

    def lo_body(it, t):
        cand = t + lax.shift_left(jnp.int32(1), 15 - it)
        cnt = _col_reduce(jnp.where(key >= cand, 1.0, 0.0), jnp.sum)
        return jnp.where(cnt >= kf, cand, t)

    thr = lax.fori_loop(0, 16, lo_body, lax.shift_left(thr_hi, 16))
    gt = key > thr
    eq = (key == thr) & adm
    need = kf - _col_reduce(jnp.where(gt, 1.0, 0.0), jnp.sum)
    eqf = jnp.where(eq, 1.0, 0.0)
    excess = _col_reduce(eqf, jnp.sum) - need

    def tie_search():
        nbits = lp.bit_length()

        def body(it, m):
            cand = m + lax.shift_left(jnp.int32(1), nbits - 1 - it)
            f = _col_reduce(jnp.where(kidx < cand, eqf, 0.0), jnp.sum)
            return jnp.where(f < need, cand, m)

        return lax.fori_loop(0, nbits, body, jnp.zeros((1, tq), jnp.int32)) + 1

    jstar = lax.cond(jnp.max(excess) > 0.0, tie_search, lambda: jnp.full((1, tq), lp, jnp.int32))
    return gt | (eq & (kidx < jstar))


def _attn_core(q, qi, kw, k_all, v_all, ki_all, x, out_prev, mod, w_out, *, past, row0, rows, n_sel, tq):
    b, t, d = x.shape
    extra_in = [] if out_prev is None else [out_prev]
    extra_spec = [] if out_prev is None else [pl.BlockSpec(memory_space=pl.ANY)]
    alias = {} if out_prev is None else {9: 0}
    l_true = past + row0 + rows
    lp = -(-l_true // LANES) * LANES
    blk0 = row0 // tq
    row3 = lambda bi, i: (bi, blk0 + i, 0)
    per_b = lambda bi, i: (bi, 0, 0)
    kern = functools.partial(_attn_core_kernel, pos0=past + row0, l_true=l_true, n_sel=n_sel)
    return pl.pallas_call(
        kern,
        name=f"attn_core_l{lp}",
        out_shape=jax.ShapeDtypeStruct((b, t, d), F32),
        input_output_aliases=alias,
        grid=(b, rows // tq),
        in_specs=[pl.BlockSpec((1, tq, Q_END), row3),
                  pl.BlockSpec((1, tq, QI_END - V_END), row3),
                  pl.BlockSpec((1, tq, LANES), row3),
                  pl.BlockSpec((1, lp, K_END - Q_END), per_b),
                  pl.BlockSpec((1, lp, V_END - K_END), per_b),
                  pl.BlockSpec((1, lp, IDX_DIM), per_b),
                  pl.BlockSpec((1, tq, d), row3),
                  pl.BlockSpec((1, 6, d), per_b),
                  _wspec((N_HEADS * HEAD_DIM, d))] + extra_spec,
        out_specs=pl.BlockSpec((1, tq, d), row3),
        compiler_params=_cparams(("parallel", "parallel")),
    )(q, qi, kw, k_all, v_all, ki_all, x, mod, w_out, *extra_in)


def _conv_mixer_kernel(x_ref, mod_ref, g_ref, win_ref, cw_ref, wout_ref, left_ref, o_ref, st_ref, carry_ref):
    d = D_MODEL
    tm = x_ref.shape[1]

    @pl.when(pl.program_id(1) == 0)
    def _():
        carry_ref[...] = jnp.zeros_like(carry_ref)
        carry_ref[6:8, :] = left_ref[0]

    x = x_ref[0]
    h = _norm_mod(x, g_ref[...], mod_ref[0, 1:2, :], mod_ref[0, 0:1, :]).astype(BF16)
    gb = _dot(h, win_ref[:, 0:d])
    gc = _dot(h, win_ref[:, d:2 * d])
    u = gc * _dot(h, win_ref[:, 2 * d:3 * d])
    y = _causal_conv3(u, carry_ref[...], cw_ref[0:1, :], cw_ref[1:2, :], cw_ref[2:3, :])
    carry_ref[...] = u[tm - SUBLANES:tm, :]
    st_ref[0] = u[tm - 2:tm, :]
    o_ref[0] = x + mod_ref[0, 2:3, :] * _dot((gb * y).astype(BF16), wout_ref[...])


def _conv_mixer(x, mod, g, w_in, conv_w, w_out, left, tm):
    b, t, d = x.shape
    row3 = lambda bi, i: (bi, i, 0)
    per_b = lambda bi, i: (bi, 0, 0)
    const2 = lambda bi, i: (0, 0)
    return pl.pallas_call(
        _conv_mixer_kernel,
        name="conv_mixer",
        out_shape=(jax.ShapeDtypeStruct((b, t, d), F32), jax.ShapeDtypeStruct((b, 2, d), F32)),
        grid=(b, t // tm),
        in_specs=[pl.BlockSpec((1, tm, d), row3),
                  pl.BlockSpec((1, 6, d), per_b),
                  pl.BlockSpec((1, d), const2),
                  _wspec((d, 3 * d)),
                  pl.BlockSpec((3, d), const2),
                  _wspec((d, d)),
                  pl.BlockSpec((1, 2, d), per_b)],
        out_specs=(pl.BlockSpec((1, tm, d), row3), pl.BlockSpec((1, 2, d), per_b)),
        scratch_shapes=[pltpu.VMEM((SUBLANES, d), F32)],
        compiler_params=_cparams(("parallel", "arbitrary")),
    )(x, mod, g, w_in, conv_w, w_out, left)


FFN_COLS = 512
FFN_ROWS = 1024


def _ffn_kernel(x_ref, mod_ref, g_ref, wup_ref, cw_ref, cb_ref, wdn_ref, left_ref, gfin_ref, o_ref, st_ref,
                carry_ref, *, final_norm):
    tm = x_ref.shape[1]

    @pl.when(pl.program_id(1) == 0)
    def _():
        carry_ref[...] = jnp.zeros_like(carry_ref)
        carry_ref[6:8, :] = left_ref[0]

    x = x_ref[0]
    h = _norm_mod(x, g_ref[...], mod_ref[0, 4:5, :], mod_ref[0, 3:4, :]).astype(BF16)

    def up_cols(c):
        return _dot(h, wup_ref[:, c:c + FFN_COLS]), _dot(h, wup_ref[:, D_FF + c:D_FF + c + FFN_COLS])

    def conv_cols(u, lo):
        z = _causal_conv3(u, carry_ref[:, lo:lo + FFN_COLS], cw_ref[0:1, lo:lo + FFN_COLS],
                          cw_ref[1:2, lo:lo + FFN_COLS], cw_ref[2:3, lo:lo + FFN_COLS]) + cb_ref[:, lo:lo + FFN_COLS]
        carry_ref[:, lo:lo + FFN_COLS] = u[tm - SUBLANES:tm, :]
        st_ref[0, :, lo:lo + FFN_COLS] = u[tm - 2:tm, :]
        return z

    acc = jnp.zeros((tm, D_MODEL), F32)
    u_next = up_cols(0)
    for c in range(0, D_FF, FFN_COLS):
        u_gate, u_val = u_next
        if c + FFN_COLS < D_FF:
            u_next = up_cols(c + FFN_COLS)
        gate = conv_cols(u_gate, c)
        val = conv_cols(u_val, D_FF + c)
        act = (gate * jax.nn.sigmoid(gate)) * val
        acc = acc + _dot(act.astype(BF16), wdn_ref[c:c + FFN_COLS, :])
    y = x + mod_ref[0, 5:6, :] * acc
    if final_norm:
        ms = jnp.mean(y * y, axis=-1, keepdims=True)
        y = y * lax.rsqrt(ms + NORM_EPS) * gfin_ref[...]
    o_ref[0] = y


def _ffn(x, mod, g, w_up, conv_w, conv_b, w_down, left, g_final, tm, final_norm):
    b, t, d = x.shape
    row3 = lambda bi, i: (bi, i, 0)
    per_b = lambda bi, i: (bi, 0, 0)
    const2 = lambda bi, i: (0, 0)
    kern = functools.partial(_ffn_kernel, final_norm=final_norm)
    return pl.pallas_call(
        kern,
        name="conv_ffn",
        out_shape=(jax.ShapeDtypeStruct((b, t, d), F32), jax.ShapeDtypeStruct((b, 2, 2 * D_FF), F32)),
        grid=(b, t // tm),
        in_specs=[pl.BlockSpec((1, tm, d), row3),
                  pl.BlockSpec((1, 6, d), per_b),
                  pl.BlockSpec((1, d), const2),
                  _wspec((d, 2 * D_FF)),
                  pl.BlockSpec((3, 2 * D_FF), const2),
                  pl.BlockSpec((1, 2 * D_FF), const2),
                  _wspec((D_FF, d)),
                  pl.BlockSpec((1, 2, 2 * D_FF), per_b),
                  pl.BlockSpec((1, d), const2)],
        out_specs=(pl.BlockSpec((1, tm, d), row3), pl.BlockSpec((1, 2, 2 * D_FF), per_b)),
        scratch_shapes=[pltpu.VMEM((SUBLANES, 2 * D_FF), F32)],
        compiler_params=_cparams(("parallel", "arbitrary")),
    )(x, mod, g, w_up, conv_w, conv_b, w_down, left, g_final)


def _softplus(z):
    return jnp.maximum(z, 0.0) + jnp.log1p(jnp.exp(-jnp.abs(z)))


def _rwkv_proj_kernel(x_ref, mod_ref, g_ref, sp_ref, mix_ref, wrkv_ref, w1_ref, w2_ref, w0_ref, a1_ref, a2_ref,
                      a0_ref, g1_ref, g2_ref, r_ref, k_ref, v_ref, w_ref, a_ref, gg_ref, sh_ref, carry_ref):
    tm = x_ref.shape[1]

    @pl.when(pl.program_id(1) == 0)
    def _():
        carry_ref[...] = jnp.zeros_like(carry_ref)
        carry_ref[7:8, :] = sp_ref[0]

    h = _norm_mod(x_ref[0], g_ref[...], mod_ref[0, 1:2, :], mod_ref[0, 0:1, :])
    (hs,) = _shifted(h, carry_ref[...], 1)
    carry_ref[...] = h[tm - SUBLANES:tm, :]
    sh_ref[0] = h[tm - 1:tm, :]
    xx = hs - h

    def mixed(j):
        return (h + xx * mix_ref[j:j + 1, :]).astype(BF16)

    r_ref[0] = _dot(mixed(0), wrkv_ref[0])
    k_ref[0] = _dot(mixed(2), wrkv_ref[1])
    v_ref[0] = _dot(mixed(3), wrkv_ref[2])
    lw = _dot(jnp.tanh(_dot(mixed(1), w1_ref[...])).astype(BF16), w2_ref[...])
    w_log = -_softplus(-(w0_ref[...] + lw)) - 0.5
    w_ref[0] = jnp.exp(-jnp.exp(w_log))
    a_ref[0] = jax.nn.sigmoid(a0_ref[...] + _dot(_dot(mixed(4), a1_ref[...]).astype(BF16), a2_ref[...]))
    gg_ref[0] = _dot(jax.nn.sigmoid(_dot(mixed(5), g1_ref[...])).astype(BF16), g2_ref[...])


def _rwkv_proj(x, mod, g, shift_prev, mix, w_rkv, w1, w2, w0, a1, a2, a0, g1, g2, tm):
    b, t, d = x.shape
    row3 = lambda bi, i: (bi, i, 0)
    per_b = lambda bi, i: (bi, 0, 0)
    const2 = lambda bi, i: (0, 0)
    big = jax.ShapeDtypeStruct((b, t, d), F32)
    lora = w1.shape[1]
    gl = g1.shape[1]
    return pl.pallas_call(
        _rwkv_proj_kernel,
        name="rwkv_proj",
        out_shape=(big,) * 6 + (jax.ShapeDtypeStruct((b, 1, d), F32),),
        grid=(b, t // tm),
        in_specs=[pl.BlockSpec((1, tm, d), row3),
                  pl.BlockSpec((1, 6, d), per_b),
                  pl.BlockSpec((1, d), const2),
                  pl.BlockSpec((1, 1, d), per_b),
                  pl.BlockSpec((6, d), const2),
                  _wspec((3, d, d)),
                  pl.BlockSpec((d, lora), const2),
                  pl.BlockSpec((lora, d), const2),
                  pl.BlockSpec((1, d), const2),
                  pl.BlockSpec((d, lora), const2),
                  pl.BlockSpec((lora, d), const2),
                  pl.BlockSpec((1, d), const2),
                  pl.BlockSpec((d, gl), const2),
                  pl.BlockSpec((gl, d), const2)],
        out_specs=(pl.BlockSpec((1, tm, d), row3),) * 6 + (pl.BlockSpec((1, 1, d), per_b),),
        scratch_shapes=[pltpu.VMEM((SUBLANES, d), F32)],
        compiler_params=_cparams(("parallel", "arbitrary")),
    )(x, mod, g, shift_prev, mix, w_rkv, w1, w2, w0, a1, a2, a0, g1, g2)


SCAN_STEPS = 64
def _rwkv_scan_kernel(r_ref, k_ref, v_ref, w_ref, a_ref, kk_ref, ka_ref, rk_ref, lg_ref, lb_ref, s0_ref,
                      y_ref, sfin_ref, s_ref, bc_ref):
    tc = r_ref.shape[0]
    n = RW_HEAD

    @pl.when(pl.program_id(1) == 0)
    def _():
        s_ref[...] = s0_ref[...]

    def unit_kk(t):
        kkr = k_ref[t] * kk_ref[...]
        nrm = jnp.sqrt(jnp.sum(kkr * kkr, axis=0, keepdims=True))
        return kkr / jnp.maximum(nrm, 1e-12)

    def step(t, carry, has_next):
        cum_prev, sa, kk = carry
        kt = k_ref[t]
        at = a_ref[t]
        rt = r_ref[t]
        vt = v_ref[t]
        cum = cum_prev * w_ref[t]
        inv = 1.0 / cum
        kmod = kt * (1.0 + (at - 1.0) * ka_ref[...])
        bc_ref[0] = (kk * at) * inv
        bc_ref[1] = kmod * inv
        bc_ref[2] = rt * cum
        if has_next:
            kk_next = unit_kk(t + 1)
            bc_ref[3] = -kk_next * cum
        y = jnp.zeros((n, LANES), F32)
        sa_next = jnp.zeros((n, LANES), F32)
        for j in range(n):
            sj = s_ref[j] + sa * bc_ref[0, j:j + 1, :] + vt * bc_ref[1, j:j + 1, :]
            s_ref[j] = sj
            y = y + sj * bc_ref[2, j:j + 1, :]
            if has_next:
                sa_next = sa_next + sj * bc_ref[3, j:j + 1, :]
        mu = jnp.mean(y, axis=0, keepdims=True)
        yc = y - mu
        var = jnp.mean(yc * yc, axis=0, keepdims=True)
        yn = (yc * lax.rsqrt(var + RW_GN_EPS)) * lg_ref[...] + lb_ref[...]
        bonus = jnp.sum(rt * kmod * rk_ref[...], axis=0, keepdims=True) * vt
        y_ref[t] = yn + bonus
        if has_next:
            return cum, sa_next, kk_next
        return cum

    kk0 = unit_kk(0)
    bc_ref[3] = -kk0
    sa0 = jnp.zeros((n, LANES), F32)
    for j in range(n):
        sa0 = sa0 + s_ref[j] * bc_ref[3, j:j + 1, :]
    carry = (jnp.ones((n, LANES), F32), sa0, kk0)
    carry = lax.fori_loop(0, tc - 1, lambda t, c: step(t, c, True), carry)
    bc_ref[0] = step(tc - 1, carry, False)
    for j in range(n):
        s_ref[j] = s_ref[j] * bc_ref[0, j:j + 1, :]
    sfin_ref[...] = s_ref[...]


def _rwkv_scan(r, k, v, w, a, kk_t, ka_t, rk_t, lg_t, lb_t, s0, tc):
    t, n, bh = r.shape
    seq = pl.BlockSpec((tc, n, LANES), lambda lb, ti: (ti, 0, lb))
    par = pl.BlockSpec((n, LANES), lambda lb, ti: (0, lb))
    st = pl.BlockSpec((n, n, LANES), lambda lb, ti: (0, 0, lb))
    return pl.pallas_call(
        _rwkv_scan_kernel,
        name="rwkv_scan",
        out_shape=(jax.ShapeDtypeStruct((t, n, bh), F32), jax.ShapeDtypeStruct((n, n, bh), F32)),
        grid=(bh // LANES, t // tc),
        in_specs=[seq] * 5 + [par] * 5 + [st],
        out_specs=(seq, st),
        scratch_shapes=[pltpu.VMEM((n, n, LANES), F32), pltpu.VMEM((5, n, LANES), F32)],
        compiler_params=_cparams(("parallel", "arbitrary")),
    )(r, k, v, w, a, kk_t, ka_t, rk_t, lg_t, lb_t, s0)


def _rwkv_out_kernel(y_ref, gg_ref, x_ref, mod_ref, wo_ref, o_ref):
    o_ref[0] = x_ref[0] + mod_ref[0, 2:3, :] * _dot((y_ref[0] * gg_ref[0]).astype(BF16), wo_ref[...])


def _rwkv_out(y, gg, x, mod, w_o, tm):
    b, t, d = x.shape
    row3 = lambda bi, i: (bi, i, 0)
    return pl.pallas_call(
        _rwkv_out_kernel,
        name="rwkv_out",
        out_shape=jax.ShapeDtypeStruct((b, t, d), F32),
        grid=(b, t // tm),
        in_specs=[pl.BlockSpec((1, tm, d), row3), pl.BlockSpec((1, tm, d), row3), pl.BlockSpec((1, tm, d), row3),
                  pl.BlockSpec((1, 6, d), lambda bi, i: (bi, 0, 0)),
                  _wspec((d, d))],
        out_specs=pl.BlockSpec((1, tm, d), row3),
        compiler_params=_cparams(("parallel", "parallel")),
    )(y, gg, x, mod, w_o)


def _pad_rows(a, rows):
    return jnp.pad(a, ((0, 0), (0, rows - a.shape[1]), (0, 0)))


def _trunk(x, mods, states, ffn_left, wts, past):
    b, t, d = x.shape
    tm = min(t, 512)
    tq = min(t, 256)
    g_norm = wts['g_norm']
    new_states = []
    new_ffn = []
    tables = _rope_tables(t, past)
    for i in range(DEPTH):
        mod = mods[i]
        kind = i % 3
        g_a = g_norm[i, 0][None, :]
        if kind == 0:
            j = i // 3
            past_k, past_v, past_ki = states[i]
            q, k, v, qi, kw, k4, v4 = _attn_proj(x, mod, g_a, wts['attn_w_main'][j], wts['attn_w_tail'][j],
                                                 tables, tm)
            ki = kw[:, :, :IDX_DIM]
            l_all = past + t
            if past:
                lp = -(-l_all // LANES) * LANES
                kvd = N_KV_HEADS * HEAD_DIM
                k_all = _pad_rows(jnp.concatenate([past_k.reshape(b, past, kvd), k], axis=1), lp)
                v_all = _pad_rows(jnp.concatenate([past_v.reshape(b, past, kvd), v], axis=1), lp)
                ki_all = _pad_rows(jnp.concatenate([past_ki, ki], axis=1), lp)
            else:
                k_all, v_all, ki_all = k, v, ki
            x_new = None
            for row0 in range(0, t, tq):
                x_new = _attn_core(q, qi, kw, k_all, v_all, ki_all, x, x_new, mod, wts['attn_w_out'][j], past=past,
                                   row0=row0, rows=tq, n_sel=min(TOPK_MAX, l_all // 4), tq=tq)
            x = x_new
            new_states.append((k4, v4, ki))
        elif kind == 1:
            x, conv_st = _conv_mixer(x, mod, g_a, wts['sc_w_in'], wts['sc_conv_w'], wts['sc_w_out'],
                                     states[i][0], tm)
            new_states.append((conv_st,))
        else:
            shift_prev, wkv0 = states[i]
            r, k, v, w, a, gg, shift_new = _rwkv_proj(
                x, mod, g_a, shift_prev, wts['rw_mix'], wts['rw_w_rkv'], wts['rw_w1'], wts['rw_w2'], wts['rw_w0'],
                wts['rw_a1'], wts['rw_a2'], wts['rw_a0'], wts['rw_g1'], wts['rw_g2'], tm)
            bh = b * RW_HEADS

            def to_scan(z):
                return z.reshape(b, t, RW_HEADS, RW_HEAD).transpose(1, 3, 0, 2).reshape(t, RW_HEAD, bh)

            def head_tile(p):
                return jnp.tile(p.reshape(RW_HEADS, RW_HEAD).T, (1, b))

            s0 = wkv0.transpose(3, 2, 0, 1).reshape(RW_HEAD, RW_HEAD, bh)
            y, s_fin = _rwkv_scan(to_scan(r), to_scan(k), to_scan(v), to_scan(w), to_scan(a),
                                  head_tile(wts['rw_k_k']), head_tile(wts['rw_k_a']), head_tile(wts['rw_r_k']),
                                  head_tile(wts['rw_ln_g']), head_tile(wts['rw_ln_b']), s0, min(t, SCAN_STEPS))
            y = y.reshape(t, RW_HEAD, b, RW_HEADS).transpose(2, 0, 3, 1).reshape(b, t, d)
            wkv_new = s_fin.reshape(RW_HEAD, RW_HEAD, b, RW_HEADS).transpose(2, 3, 1, 0)
            x = _rwkv_out(y, gg, x, mod, wts['rw_w_o'], tm)
            new_states.append((shift_new, wkv_new))
        x, f_st = _ffn(x, mod, g_norm[i, 1][None, :], wts['ffn_w_up'][i], wts['ffn_conv_w'][i],
                       wts['ffn_conv_b'][i][None, :], wts['ffn_w_down'][i], ffn_left[i],
                       wts['g_final'][None, :], min(t, FFN_ROWS), i == DEPTH - 1)
        new_ffn.append(f_st)
    return x, new_states, jnp.stack(new_ffn)


def _pad_cols(w, cols):
    return jnp.pad(w, ((0, 0), (0, cols - w.shape[1])))


def _pad_rows2(w, rows):
    return jnp.pad(w, ((0, rows - w.shape[0]), (0, 0)))


def kernel(x_prompt, x_sample, c_prompt, c_sample, cache_k_0, cache_v_0, cache_kidx_0, state_conv_1, state_shift_2, state_wkv_2, cache_k_3, cache_v_3, cache_kidx_3, state_ffn_conv, w_mod, b_mod, g_norm, g_final, attn_w_in, attn_w_out, sc_w_in, sc_conv_w, sc_w_out, rw_mix, rw_w_rkv, rw_w_o, rw_w0, rw_w1, rw_w2, rw_a0, rw_a1, rw_a2, rw_g1, rw_g2, rw_k_k, rw_k_a, rw_r_k, rw_ln_g, rw_ln_b, ffn_w_up, ffn_conv_w, ffn_conv_b, ffn_w_down):
    d = D_MODEL
    bp = x_prompt.shape[0]
    bs = x_sample.shape[0]
    dt = x_prompt.dtype

    wts = _prep_weights(g_norm, g_final, attn_w_in, attn_w_out, sc_w_in, sc_conv_w, sc_w_out, rw_mix, rw_w_rkv,
                        rw_w_o, rw_w0, rw_w1, rw_w2, rw_a0, rw_a1, rw_a2, rw_g1, rw_g2, rw_k_k, rw_k_a, rw_r_k,
                        rw_ln_g, rw_ln_b, ffn_w_up, ffn_conv_w, ffn_conv_b, ffn_w_down)

    mods = _mod_all(jnp.concatenate([c_prompt, c_sample], axis=0), w_mod, b_mod)
    mods_p = mods[:, :bp].reshape(DEPTH, bp, 6, d)
    mods_s = mods[:, bp:].reshape(DEPTH, bs, 6, d)

    ffn0 = jnp.zeros((DEPTH, bp, 2, 2 * D_FF), dt)
    y_p, st_p, ffn_p = _trunk(x_prompt, mods_p, _empty_states(bp, dt), ffn0, wts, 0)
    sample_states = [(cache_k_0, cache_v_0, cache_kidx_0), (state_conv_1,), (state_shift_2, state_wkv_2),
                     (cache_k_3, cache_v_3, cache_kidx_3)]
    y_s, st_s, ffn_s = _trunk(x_sample, mods_s, sample_states, state_ffn_conv, wts, cache_k_0.shape[1])
    (k0_p, v0_p, ki0_p), (conv1_p,), (shift2_p, wkv2_p), (k3_p, v3_p, ki3_p) = st_p
    (k0_s, v0_s, ki0_s), (conv1_s,), (shift2_s, wkv2_s), (k3_s, v3_s, ki3_s) = st_s
    return (y_p, y_s,
            k0_p, v0_p, ki0_p, conv1_p, shift2_p, wkv2_p, k3_p, v3_p, ki3_p, ffn_p,
            k0_s, v0_s, ki0_s, conv1_s, shift2_s, wkv2_s, k3_s, v3_s, ki3_s, ffn_s)


def _empty_states(b, dt):
    d = D_MODEL
    kv = (jnp.zeros((b, 0, N_KV_HEADS, HEAD_DIM), dt), jnp.zeros((b, 0, N_KV_HEADS, HEAD_DIM), dt),
          jnp.zeros((b, 0, IDX_DIM), dt))
    return [kv, (jnp.zeros((b, 2, d), dt),),
            (jnp.zeros((b, 1, d), dt), jnp.zeros((b, RW_HEADS, RW_HEAD, RW_HEAD), dt)), kv]


def _prep_weights(g_norm, g_final, attn_w_in, attn_w_out, sc_w_in, sc_conv_w, sc_w_out, rw_mix, rw_w_rkv, rw_w_o,
                  rw_w0, rw_w1, rw_w2, rw_a0, rw_a1, rw_a2, rw_g1, rw_g2, rw_k_k, rw_k_a, rw_r_k, rw_ln_g, rw_ln_b,
                  ffn_w_up, ffn_conv_w, ffn_conv_b, ffn_w_down):
    return {
        'g_norm': g_norm, 'g_final': g_final,
        'attn_w_main': attn_w_in[:, :, :QI_END].astype(BF16),
        'attn_w_tail': jnp.pad(attn_w_in[:, :, QI_END:], ((0, 0), (0, 0), (0, LANES - (WI_END - QI_END)))).astype(BF16),
        'attn_w_out': attn_w_out.astype(BF16),
        'sc_w_in': sc_w_in.astype(BF16), 'sc_conv_w': sc_conv_w, 'sc_w_out': sc_w_out.astype(BF16),
        'rw_mix': rw_mix, 'rw_w_rkv': rw_w_rkv.astype(BF16), 'rw_w_o': rw_w_o.astype(BF16),
        'rw_w0': rw_w0[None, :], 'rw_w1': _pad_cols(rw_w1, LANES).astype(BF16),
        'rw_w2': _pad_rows2(rw_w2, LANES).astype(BF16),
        'rw_a0': rw_a0[None, :], 'rw_a1': _pad_cols(rw_a1, LANES).astype(BF16),
        'rw_a2': _pad_rows2(rw_a2, LANES).astype(BF16),
        'rw_g1': rw_g1.astype(BF16), 'rw_g2': rw_g2.astype(BF16),
        'rw_k_k': rw_k_k, 'rw_k_a': rw_k_a, 'rw_r_k': rw_r_k, 'rw_ln_g': rw_ln_g, 'rw_ln_b': rw_ln_b,
        'ffn_w_up': ffn_w_up.astype(BF16), 'ffn_conv_w': ffn_conv_w, 'ffn_conv_b': ffn_conv_b,
        'ffn_w_down': ffn_w_down.astype(BF16),
    }
```
